```python
import jax
import jax.numpy as jnp
from jax import lax
import numpy as np

D_MODEL = 1024
BATCH = 8
SEQ = 2048
DEPTH = 4
DEC_BATCH = 128
DEC_SEQ = 4
PAST_LEN = 2048
PAGE_SIZE = 128

HEAD_DIM = 64
N_HEADS = D_MODEL // HEAD_DIM
N_A_LAYERS = DEPTH // 2
N_B_LAYERS = DEPTH - N_A_LAYERS
D_FF = ((8 * D_MODEL // 3 + 127) // 128) * 128
LORA_W = 64
LORA_A = 64
LORA_V = 32
LORA_G = 128
Q_BLOCK = 128
LN_EPS = 1e-5
GN_EPS = 64e-5
SB_SCALE = HEAD_DIM ** -0.5
DEEPNORM_ALPHA = (2 * DEPTH) ** 0.25
DEEPNORM_BETA = (8 * DEPTH) ** -0.25

kernel_name = 'rwkv7_stickbreak_yoco_decoder_step'


def layer_norm(x, g, b):
    xf = x.astype(jnp.float32)
    mu = jnp.mean(xf, axis=-1, keepdims=True)
    var = jnp.mean(jnp.square(xf - mu), axis=-1, keepdims=True)
    return ((xf - mu) * lax.rsqrt(var + LN_EPS) * g.astype(jnp.float32) + b.astype(jnp.float32)).astype(x.dtype)


def post_norm(x, sub, li, slot, P):
    return layer_norm(DEEPNORM_ALPHA * x + sub, P['ln_g'][li, slot], P['ln_b'][li, slot])


def ffn_half(x, li, j, P):
    h = jax.nn.silu(x @ P['ffn_w_gate'][li, j]) * (x @ P['ffn_w_up'][li, j])
    return post_norm(x, 0.5 * (h @ P['ffn_w_down'][li, j]), li, 2 * j, P)


def rwkv7_scan(r, decay, k, v, kk, a, s0):
    def step(s, inp):
        r_t, d_t, k_t, v_t, kk_t, a_t = inp
        sa = jnp.einsum('bhvk,bhk->bhv', s, -kk_t)
        s = (s * d_t[:, :, None, :]
             + sa[..., None] * (kk_t * a_t)[:, :, None, :]
             + v_t[..., None] * k_t[:, :, None, :])
        y = jnp.einsum('bhvk,bhk->bhv', s, r_t)
        return s, y
    xs = tuple(jnp.swapaxes(t, 0, 1) for t in (r, decay, k, v, kk, a))
    s_final, ys = lax.scan(step, s0, xs)
    return jnp.swapaxes(ys, 0, 1), s_final


def rwkv7_time_mix(x, x_prev, s0, v_first, li, P):
    B, T, D = x.shape
    H, Dh = N_HEADS, HEAD_DIM
    f32 = jnp.float32
    x_shift = jnp.concatenate([x_prev[:, None, :].astype(x.dtype), x[:, :-1, :]], axis=1)
    xx = x_shift - x
    mu = P['tm_mu'][li]
    xr, xw, xk, xv, xa, xg = (x + xx * mu[i] for i in range(6))
    r = xr @ P['tm_w_r'][li]
    w_in = (P['tm_w0'][li] + jnp.tanh(xw @ P['tm_w1'][li]) @ P['tm_w2'][li]).astype(f32)
    w_log = -jax.nn.softplus(-w_in) - 0.5
    decay = jnp.exp(-jnp.exp(w_log))
    k = xk @ P['tm_w_k'][li]
    v = xv @ P['tm_w_v'][li]
    if li == 0:
        v_first = v
    else:
        j = li - 1
        v = v + (v_first - v) * jax.nn.sigmoid(P['tm_v0'][j] + (xv @ P['tm_v1'][j]) @ P['tm_v2'][j])
    a = jax.nn.sigmoid(P['tm_a0'][li] + (xa @ P['tm_a1'][li]) @ P['tm_a2'][li])
    g = jax.nn.sigmoid(xg @ P['tm_g1'][li]) @ P['tm_g2'][li]

    def heads(t):
        return t.reshape(B, T, H, Dh).astype(f32)

    kk = heads(k * P['tm_k_k'][li])
    kk = kk / jnp.maximum(jnp.linalg.norm(kk, axis=-1, keepdims=True), 1e-12)
    k_h = heads(k * (1 + (a - 1) * P['tm_k_a'][li]))
    r_h, v_h, a_h = heads(r), heads(v), heads(a)
    y, s_new = rwkv7_scan(r_h, heads(decay), k_h, v_h, kk, a_h, s0.astype(f32))
    mu_y = jnp.mean(y, axis=-1, keepdims=True)
    var_y = jnp.mean(jnp.square(y - mu_y), axis=-1, keepdims=True)
    y = ((y - mu_y) * lax.rsqrt(var_y + GN_EPS) * P['tm_gn_g'][li].reshape(H, Dh).astype(f32)
         + P['tm_gn_b'][li].reshape(H, Dh).astype(f32))
    y = y + jnp.sum(r_h * k_h * P['tm_r_k'][li].astype(f32), axis=-1, keepdims=True) * v_h
    y = y.reshape(B, T, D).astype(x.dtype)
    out = (y * g) @ P['tm_w_o'][li]
    return out, s_new.astype(s0.dtype), x[:, -1, :], v_first


def stick_breaking(q, k, v, bias, q_pos, k_pos):
    z = (jnp.einsum('bqhd,bkhd->bhqk', q, k).astype(jnp.float32) * SB_SCALE
         + bias.astype(jnp.float32)[None, :, None, None])
    mask = k_pos[None, :] < q_pos[:, None]
    log_1mb = jnp.where(mask, jax.nn.log_sigmoid(-z), 0.0)
    log_surv = lax.cumsum(log_1mb, axis=3, reverse=True) - log_1mb
    weights = jnp.where(mask, jnp.exp(jax.nn.log_sigmoid(z) + log_surv), 0.0)
    return jnp.einsum('bhqk,bkhd->bqhd', weights.astype(v.dtype), v)


def attend_prompt(q, k, v, bias):
    B, S, H, Dh = q.shape
    n_blk = S // Q_BLOCK
    qb = jnp.swapaxes(q.reshape(B, n_blk, Q_BLOCK, H, Dh), 0, 1)
    k_pos = jnp.arange(S)

    def one_block(args):
        q_blk, start = args
        return stick_breaking(q_blk, k, v, bias, start + jnp.arange(Q_BLOCK), k_pos)

    ob = lax.map(one_block, (qb, jnp.arange(n_blk) * Q_BLOCK))
    return jnp.swapaxes(ob, 0, 1).reshape(B, S, H, Dh)


def trunk(x, wkv_in, shift_in, attend, P):
    B, T, D = x.shape
    new_wkv, new_shift = [], []
    v_first = None
    k_sh = v_sh = None
    for li in range(DEPTH):
        x = ffn_half(x, li, 0, P)
        if li < N_A_LAYERS:
            out, s_new, x_last, v_first = rwkv7_time_mix(x, shift_in[li], wkv_in[li], v_first, li, P)
            new_wkv.append(s_new)
            new_shift.append(x_last)
        else:
            j = li - N_A_LAYERS
            q = (x @ P['sb_w_q'][j]).reshape(B, T, N_HEADS, HEAD_DIM)
            o = attend(q, k_sh, v_sh, P['sb_bias'][j]).reshape(B, T, D)
            out = o @ P['sb_w_o'][j]
        x = post_norm(x, out, li, 1, P)
        x = ffn_half(x, li, 1, P)
        if li == N_A_LAYERS - 1:
            k_sh = (x @ P['sb_w_k']).reshape(B, T, N_HEADS, HEAD_DIM)
            v_sh = (x @ P['sb_w_v']).reshape(B, T, N_HEADS, HEAD_DIM)
    return x, jnp.stack(new_wkv), jnp.stack(new_shift), k_sh, v_sh


def setup_inputs(seed: int = 0) -> dict:
    key = jax.random.key(seed)
    ks = iter(jax.random.split(key, 48))
    f32 = jnp.float32

    def nrm(shape, scale):
        return jax.random.normal(next(ks), shape, f32) * scale

    def uni(shape, lo, hi):
        return jax.random.uniform(next(ks), shape, f32, lo, hi)

    D, H, Dh, F = D_MODEL, N_HEADS, HEAD_DIM, D_FF
    NA, NB = N_A_LAYERS, N_B_LAYERS
    n_pages = PAST_LEN // PAGE_SIZE
    n_used = DEC_BATCH * n_pages
    n_phys = n_used + max(1, n_used // 4)
    perm = jax.random.permutation(next(ks), n_phys)
    page_table = perm[:n_used].reshape(DEC_BATCH, n_pages).astype(jnp.int32)
    sd = D ** -0.5
    return {
        'x_prompt': nrm((BATCH, SEQ, D), 1.0),
        'x_sample': nrm((DEC_BATCH, DEC_SEQ, D), 1.0),
        'cache_k': nrm((n_phys, PAGE_SIZE, H, Dh), 1.0),
        'cache_v': nrm((n_phys, PAGE_SIZE, H, Dh), 1.0),
        'state_wkv': nrm((NA, DEC_BATCH, H, Dh, Dh), 0.3),
        'state_shift': nrm((NA, DEC_BATCH, D), 1.0),
        'page_table': page_table,
        'ln_g': 1.0 + nrm((DEPTH, 3, D), 0.02),
        'ln_b': nrm((DEPTH, 3, D), 0.02),
        'ffn_w_gate': nrm((DEPTH, 2, D, F), sd),
        'ffn_w_up': nrm((DEPTH, 2, D, F), sd),
        'ffn_w_down': nrm((DEPTH, 2, F, D), F ** -0.5 * DEEPNORM_BETA),
        'tm_mu': uni((NA, 6, D), 0.0, 1.0),
        'tm_w0': uni((NA, D), -6.5, -1.5),
        'tm_w1': nrm((NA, D, LORA_W), sd),
        'tm_w2': nrm((NA, LORA_W, D), 0.1),
        'tm_a0': nrm((NA, D), 0.1),
        'tm_a1': nrm((NA, D, LORA_A), sd),
        'tm_a2': nrm((NA, LORA_A, D), 0.1),
        'tm_v0': nrm((NA - 1, D), 0.1),
        'tm_v1': nrm((NA - 1, D, LORA_V), sd),
        'tm_v2': nrm((NA - 1, LORA_V, D), 0.1),
        'tm_g1': nrm((NA, D, LORA_G), sd),
        'tm_g2': nrm((NA, LORA_G, D), LORA_G ** -0.5),
        'tm_k_k': 0.85 + nrm((NA, D), 0.02),
        'tm_k_a': 1.0 + nrm((NA, D), 0.02),
        'tm_r_k': nrm((NA, H, Dh), 0.1),
        'tm_w_r': nrm((NA, D, D), sd),
        'tm_w_k': nrm((NA, D, D), sd),
        'tm_w_v': nrm((NA, D, D), sd),
        'tm_w_o': nrm((NA, D, D), sd * DEEPNORM_BETA),
        'tm_gn_g': 1.0 + nrm((NA, D), 0.02),
        'tm_gn_b': nrm((NA, D), 0.02),
        'sb_w_k': nrm((D, D), sd),
        'sb_w_v': nrm((D, D), sd),
        'sb_w_q': nrm((NB, D, D), sd),
        'sb_bias': uni((NB, H), -8.0, -5.0),
        'sb_w_o': nrm((NB, D, D), sd * DEEPNORM_BETA),
    }


def reference(x_prompt, x_sample, cache_k, cache_v, state_wkv, state_shift, page_table,
              ln_g, ln_b, ffn_w_gate, ffn_w_up, ffn_w_down,
              tm_mu, tm_w0, tm_w1, tm_w2, tm_a0, tm_a1, tm_a2, tm_v0, tm_v1, tm_v2,
              tm_g1, tm_g2, tm_k_k, tm_k_a, tm_r_k, tm_w_r, tm_w_k, tm_w_v, tm_w_o,
              tm_gn_g, tm_gn_b, sb_w_k, sb_w_v, sb_w_q, sb_bias, sb_w_o):
    P = dict(ln_g=ln_g, ln_b=ln_b, ffn_w_gate=ffn_w_gate, ffn_w_up=ffn_w_up, ffn_w_down=ffn_w_down,
             tm_mu=tm_mu, tm_w0=tm_w0, tm_w1=tm_w1, tm_w2=tm_w2, tm_a0=tm_a0, tm_a1=tm_a1, tm_a2=tm_a2,
             tm_v0=tm_v0, tm_v1=tm_v1, tm_v2=tm_v2, tm_g1=tm_g1, tm_g2=tm_g2, tm_k_k=tm_k_k,
             tm_k_a=tm_k_a, tm_r_k=tm_r_k, tm_w_r=tm_w_r, tm_w_k=tm_w_k, tm_w_v=tm_w_v, tm_w_o=tm_w_o,
             tm_gn_g=tm_gn_g, tm_gn_b=tm_gn_b, sb_w_k=sb_w_k, sb_w_v=sb_w_v, sb_w_q=sb_w_q,
             sb_bias=sb_bias, sb_w_o=sb_w_o)

    Bp, Sp, D = x_prompt.shape
    wkv0 = jnp.zeros((N_A_LAYERS, Bp, N_HEADS, HEAD_DIM, HEAD_DIM), jnp.float32)
    shift0 = jnp.zeros((N_A_LAYERS, Bp, D), x_prompt.dtype)
    y_prompt, wkv_prompt, shift_prompt, k_p, v_p = trunk(x_prompt, wkv0, shift0, attend_prompt, P)
    k_prompt = k_p.reshape(Bp, Sp // PAGE_SIZE, PAGE_SIZE, N_HEADS, HEAD_DIM)
    v_prompt = v_p.reshape(Bp, Sp // PAGE_SIZE, PAGE_SIZE, N_HEADS, HEAD_DIM)

    Bs, n_pages = page_table.shape
    page = cache_k.shape[1]
    past_len = n_pages * page
    k_past = cache_k[page_table].reshape(Bs, past_len, N_HEADS, HEAD_DIM)
    v_past = cache_v[page_table].reshape(Bs, past_len, N_HEADS, HEAD_DIM)

    def attend_sample(q, k_new, v_new, bias):
        k_all = jnp.concatenate([k_past, k_new.astype(k_past.dtype)], axis=1)
        v_all = jnp.concatenate([v_past, v_new.astype(v_past.dtype)], axis=1)
        q_pos = past_len + jnp.arange(q.shape[1])
        k_pos = jnp.arange(k_all.shape[1])
        return stick_breaking(q, k_all, v_all, bias, q_pos, k_pos)

    y_sample, wkv_sample, shift_sample, k_sample, v_sample = trunk(
        x_sample, state_wkv, state_shift, attend_sample, P)

    return (y_prompt, y_sample, wkv_prompt, shift_prompt, k_prompt, v_prompt,
            wkv_sample, shift_sample, k_sample, v_sample)
```

```python
import functools

import jax
import jax.numpy as jnp
from jax import lax
from jax.experimental import pallas as pl
from jax.experimental.pallas import tpu as pltpu

HEAD_DIM = 64
LANES = 128
LN_EPS = 1e-5
GN_EPS = 64e-5
SB_SCALE = HEAD_DIM ** -0.5
PAGE_SIZE = 128

FFN_CHUNK = 256
TOKEN_TILE = 512
PRE_TILE = 256
SCAN_CHUNK = 64
SAMPLE_PAD = 8
ATT_BLOCK = 128
VMEM_LIMIT = 56 * 1024 * 1024

F32 = jnp.float32
BF16 = jnp.bfloat16
HI = lax.Precision.HIGHEST
NT = (((1,), (1,)), ((), ()))
TN = (((0,), (0,)), ((), ()))


def _const_spec(shape):
    nd = len(shape)
    return pl.BlockSpec(shape, lambda *_: (0,) * nd, pipeline_mode=pl.Buffered(1))


def _params(sem):
    return pltpu.CompilerParams(dimension_semantics=sem, vmem_limit_bytes=VMEM_LIMIT)


def _layer_norm(y, g, b):
    mu = jnp.mean(y, axis=-1, keepdims=True)
    yc = y - mu
    var = jnp.mean(yc * yc, axis=-1, keepdims=True)
    return yc * lax.rsqrt(var + LN_EPS) * g + b


def _sigmoid(x):
    return 1.0 / (1.0 + jnp.exp(-x))


def _split_bf16(x):
    hi = x.astype(BF16)
    lo = (x - hi.astype(F32)).astype(BF16)
    return hi, lo


def _head_sum(q, sel, sel_t):
    hi, lo = _split_bf16(q)
    s = jnp.dot(hi, sel, preferred_element_type=F32) + jnp.dot(lo, sel, preferred_element_type=F32)
    hi, lo = _split_bf16(s)
    return jnp.dot(hi, sel_t, preferred_element_type=F32) + jnp.dot(lo, sel_t, preferred_element_type=F32)


def _ffn_kernel(x_ref, wg_ref, wu_ref, wd_ref, g_ref, b_ref, o_ref, acc_ref, *, alpha, n_chunks):
    x = x_ref[...]
    xb = x.astype(BF16)
    acc_ref[...] = jnp.zeros_like(acc_ref)

    def body(c, carry):
        gate = jnp.dot(xb, wg_ref[c], preferred_element_type=F32)
        up = jnp.dot(xb, wu_ref[c], preferred_element_type=F32)
        h = (gate * _sigmoid(gate) * up).astype(BF16)
        acc_ref[...] += jnp.dot(h, wd_ref[c], preferred_element_type=F32)
        return carry

    lax.fori_loop(0, n_chunks, body, 0)
    y = alpha * x + 0.5 * acc_ref[...]
    o_ref[...] = _layer_norm(y, g_ref[...], b_ref[...])


def _ffn_half(x, wg, wu, wd, g, b, alpha):
    n, d = x.shape
    n_chunks = wg.shape[0]
    tm = TOKEN_TILE
    kern = functools.partial(_ffn_kernel, alpha=alpha, n_chunks=n_chunks)
    return pl.pallas_call(
        kern,
        grid=(n // tm,),
        in_specs=[
            pl.BlockSpec((tm, d), lambda i: (i, 0)),
            _const_spec(wg.shape), _const_spec(wu.shape), _const_spec(wd.shape),
            _const_spec(g.shape), _const_spec(b.shape),
        ],
        out_specs=pl.BlockSpec((tm, d), lambda i: (i, 0)),
        out_shape=jax.ShapeDtypeStruct((n, d), F32),
        scratch_shapes=[pltpu.VMEM((tm, d), F32)],
        compiler_params=_params(("parallel",)),
        name="ffn_half",
    )(x, wg, wu, wd, g, b)


def _tm_pre_kernel(*refs, has_vres):
    if has_vres:
        (x_ref, xs_ref, vf_ref, mu_ref, wr_ref, wk_ref, wv_ref, w0_ref, w1_ref, w2_ref,
         a0_ref, a1_ref, a2_ref, v0_ref, v1_ref, v2_ref, g1_ref, g2_ref, kk_ref, ka_ref,
         sel_ref, selt_ref,
         r_out, lw_out, k_out, v_out, a_out, b_out, g_out) = refs
    else:
        (x_ref, xs_ref, mu_ref, wr_ref, wk_ref, wv_ref, w0_ref, w1_ref, w2_ref,
         a0_ref, a1_ref, a2_ref, g1_ref, g2_ref, kk_ref, ka_ref,
         sel_ref, selt_ref,
         r_out, lw_out, k_out, v_out, a_out, b_out, g_out) = refs
    x = x_ref[...]
    xx = xs_ref[...] - x

    def mix(i):
        return (x + xx * mu_ref[i:i + 1, :]).astype(BF16)

    def mm(a, w_ref):
        return jnp.dot(a, w_ref[...], preferred_element_type=F32)

    xr, xw, xk, xv, xa, xg = (mix(i) for i in range(6))
    r_out[...] = mm(xr, wr_ref)
    w_in = w0_ref[...] + mm(jnp.tanh(mm(xw, w1_ref)).astype(BF16), w2_ref)
    softplus = jnp.maximum(-w_in, 0.0) + jnp.log1p(jnp.exp(-jnp.abs(w_in)))
    lw_out[...] = -jnp.exp(-softplus - 0.5)
    k = mm(xk, wk_ref)
    v = mm(xv, wv_ref)
    if has_vres:
        gate = _sigmoid(v0_ref[...] + mm(mm(xv, v1_ref).astype(BF16), v2_ref))
        v = v + (vf_ref[...] - v) * gate
    v_out[...] = v
    a = _sigmoid(a0_ref[...] + mm(mm(xa, a1_ref).astype(BF16), a2_ref))
    g_out[...] = mm(_sigmoid(mm(xg, g1_ref)).astype(BF16), g2_ref)
    kk = k * kk_ref[...]
    norm = jnp.sqrt(_head_sum(kk * kk, sel_ref[...], selt_ref[...]))
    kk = kk / jnp.maximum(norm, 1e-12)
    k_out[...] = k * (1.0 + (a - 1.0) * ka_ref[...])
    a_out[...] = -kk
    b_out[...] = kk * a


def _tm_pre(x, xs, v_first, weights):
    n, d = x.shape
    tm = PRE_TILE
    has_vres = v_first is not None
    tile = pl.BlockSpec((tm, d), lambda i: (i, 0))
    acts = [x, xs] + ([v_first] if has_vres else [])
    kern = functools.partial(_tm_pre_kernel, has_vres=has_vres)
    out = jax.ShapeDtypeStruct((n, d), F32)
    return pl.pallas_call(
        kern,
        grid=(n // tm,),
        in_specs=[tile] * len(acts) + [_const_spec(w.shape) for w in weights],
        out_specs=[tile] * 7,
        out_shape=[out] * 7,
        compiler_params=_params(("parallel",)),
        name="tm_pre_vres" if has_vres else "tm_pre",
    )(*acts, *weights)


def _scan_kernel(r_ref, lw_ref, k_ref, v_ref, a_ref, b_ref, s0_ref, y_ref, sout_ref, s_scr,
                 *, chunk, n_chunks):
    c = pl.program_id(2)
    hd = HEAD_DIM

    @pl.when(c == 0)
    def _():
        s_scr[...] = jnp.zeros_like(s_scr)
        s_scr[0:hd, 0:hd] = s0_ref[0, 0]
        s_scr[hd:2 * hd, hd:2 * hd] = s0_ref[0, 1]

    def dot(a, b_):
        return jnp.dot(a, b_, preferred_element_type=F32, precision=HI)

    def dot_nt(a, b_):
        return lax.dot_general(a, b_, NT, preferred_element_type=F32, precision=HI)

    def dot_tn(a, b_):
        return lax.dot_general(a, b_, TN, preferred_element_type=F32, precision=HI)

    r, lw, k, v, a, b = (ref[...] for ref in (r_ref, lw_ref, k_ref, v_ref, a_ref, b_ref))
    row = lax.broadcasted_iota(jnp.int32, (chunk, chunk), 0)
    col = lax.broadcasted_iota(jnp.int32, (chunk, chunk), 1)
    strict = col < row
    incl = col <= row
    cum = dot(incl.astype(F32), lw)
    w_last = cum[chunk - 1:chunk, :]
    rt = r * jnp.exp(cum)
    at = a * jnp.exp(cum - lw)
    w_neg = jnp.exp(-cum)
    kt = k * w_neg
    bt = b * w_neg
    w_rest = jnp.exp(w_last - cum)
    kh = k * w_rest
    bh = b * w_rest

    lane = lax.broadcasted_iota(jnp.int32, (chunk, LANES), 1)
    head0 = lane < hd
    s = s_scr[...]
    u_rhs0 = dot_nt(at, s)
    y0 = dot_nt(rt, s)

    heads = []
    for mh in (head0, lane >= hd):
        atm = jnp.where(mh, at, 0.0)
        rtm = jnp.where(mh, rt, 0.0)
        a_ab = jnp.where(strict, dot_nt(atm, bt), 0.0)
        a_ak = jnp.where(strict, dot_nt(atm, kt), 0.0)
        a_rb = jnp.where(incl, dot_nt(rtm, bt), 0.0)
        a_rk = jnp.where(incl, dot_nt(rtm, kt), 0.0)
        t_inv = jnp.where(row == col, 1.0, a_ab)
        power = a_ab
        span = 2
        while span < chunk:
            power = dot(power, power)
            t_inv = t_inv + dot(t_inv, power)
            span *= 2
        heads.append((a_ak, a_rb, a_rk, t_inv))

    (ak0, rb0, rk0, ti0), (ak1, rb1, rk1, ti1) = heads
    u_rhs = u_rhs0 + jnp.where(head0, dot(ak0, v), dot(ak1, v))
    u = jnp.where(head0, dot(ti0, u_rhs), dot(ti1, u_rhs))
    y = y0 + jnp.where(head0, dot(rk0, v) + dot(rb0, u), dot(rk1, v) + dot(rb1, u))
    y_ref[...] = y

    srow = lax.broadcasted_iota(jnp.int32, (LANES, LANES), 0)
    scol = lax.broadcasted_iota(jnp.int32, (LANES, LANES), 1)
    same_head = (srow // hd) == (scol // hd)
    s_new = s * jnp.exp(w_last) + jnp.where(same_head, dot_tn(v, kh) + dot_tn(u, bh), 0.0)
    s_scr[...] = s_new

    @pl.when(c == n_chunks - 1)
    def _():
        sout_ref[0, 0] = s_new[0:hd, 0:hd]
        sout_ref[0, 1] = s_new[hd:2 * hd, hd:2 * hd]


def _rwkv_scan(r, lw, k, v, a, b, s0, chunk):
    n, d = r.shape
    n_seq, n_heads = s0.shape[0], s0.shape[1]
    t = n // n_seq
    n_chunks = t // chunk
    n_pairs = d // LANES
    tile = pl.BlockSpec((chunk, LANES), lambda bi, p, c: (bi * n_chunks + c, p))
    st = pl.BlockSpec((1, 2, HEAD_DIM, HEAD_DIM), lambda bi, p, c: (bi, p, 0, 0))
    kern = functools.partial(_scan_kernel, chunk=chunk, n_chunks=n_chunks)
    return pl.pallas_call(
        kern,
        grid=(n_seq, n_pairs, n_chunks),
        in_specs=[tile] * 6 + [st],
        out_specs=[tile, st],
        out_shape=[jax.ShapeDtypeStruct((n, d), F32), jax.ShapeDtypeStruct(s0.shape, F32)],
        scratch_shapes=[pltpu.VMEM((LANES, LANES), F32)],
        compiler_params=_params(("parallel", "parallel", "arbitrary")),
        name=f"rwkv_scan_c{chunk}",
    )(r, lw, k, v, a, b, s0)


def _tm_post_kernel(y_ref, r_ref, k_ref, v_ref, g_ref, x_ref, wo_ref, gng_ref, gnb_ref, rk_ref,
                    lng_ref, lnb_ref, sel_ref, selt_ref, o_ref, *, alpha):
    sel, selt = sel_ref[...], selt_ref[...]
    y = y_ref[...]
    inv = 1.0 / HEAD_DIM
    yc = y - _head_sum(y, sel, selt) * inv
    var = _head_sum(yc * yc, sel, selt) * inv
    yn = yc * lax.rsqrt(var + GN_EPS) * gng_ref[...] + gnb_ref[...]
    bonus = _head_sum(r_ref[...] * k_ref[...] * rk_ref[...], sel, selt) * v_ref[...]
    z = ((yn + bonus) * g_ref[...]).astype(BF16)
    out = jnp.dot(z, wo_ref[...], preferred_element_type=F32)
    o_ref[...] = _layer_norm(alpha * x_ref[...] + out, lng_ref[...], lnb_ref[...])


def _tm_post(y, r, k, v, g, x, weights, alpha):
    n, d = x.shape
    tm = PRE_TILE
    tile = pl.BlockSpec((tm, d), lambda i: (i, 0))
    return pl.pallas_call(
        functools.partial(_tm_post_kernel, alpha=alpha),
        grid=(n // tm,),
        in_specs=[tile] * 6 + [_const_spec(w.shape) for w in weights],
        out_specs=tile,
        out_shape=jax.ShapeDtypeStruct((n, d), F32),
        compiler_params=_params(("parallel",)),
        name="tm_post",
    )(y, r, k, v, g, x, *weights)


def _proj_kernel(x_ref, *refs):
    n_out = len(refs) // 2
    xb = x_ref[...].astype(BF16)
    for w_ref, o_ref in zip(refs[:n_out], refs[n_out:]):
        o_ref[...] = jnp.dot(xb, w_ref[...], preferred_element_type=F32)


def _proj(x, ws, name):
    n, d = x.shape
    tm = TOKEN_TILE
    tile = pl.BlockSpec((tm, d), lambda i: (i, 0))
    outs = pl.pallas_call(
        _proj_kernel,
        grid=(n // tm,),
        in_specs=[tile] + [_const_spec(w.shape) for w in ws],
        out_specs=[pl.BlockSpec((tm, w.shape[1]), lambda i: (i, 0)) for w in ws],
        out_shape=[jax.ShapeDtypeStruct((n, w.shape[1]), F32) for w in ws],
        compiler_params=_params(("parallel",)),
        name=name,
    )(x, *ws)
    return outs


def _oproj_kernel(o_ref, x_ref, w_ref, g_ref, b_ref, out_ref, *, alpha):
    sub = jnp.dot(o_ref[...].astype(BF16), w_ref[...], preferred_element_type=F32)
    out_ref[...] = _layer_norm(alpha * x_ref[...] + sub, g_ref[...], b_ref[...])


def _oproj_postnorm(o, x, w, g, b, alpha):
    n, d = x.shape
    tm = TOKEN_TILE
    tile = pl.BlockSpec((tm, d), lambda i: (i, 0))
    return pl.pallas_call(
        functools.partial(_oproj_kernel, alpha=alpha),
        grid=(n // tm,),
        in_specs=[tile, tile, _const_spec(w.shape), _const_spec(g.shape), _const_spec(b.shape)],
        out_specs=tile,
        out_shape=jax.ShapeDtypeStruct((n, d), F32),
        compiler_params=_params(("parallel",)),
        name="oproj_postnorm",
    )(o, x, w, g, b)


def _log_sigmoids(z):
    sp = jnp.log1p(jnp.exp(-jnp.abs(z)))
    neg = -(jnp.maximum(z, 0.0) + sp)
    return neg, neg + z


def _attn_prompt_kernel(bias_ref, q_ref, k_ref, v_ref, o_ref, *, blk):
    p = pl.program_id(1)
    qb = pl.program_id(2)
    hd = HEAD_DIM
    q = q_ref[...] * SB_SCALE
    lane = lax.broadcasted_iota(jnp.int32, (blk, LANES), 1)
    head0 = lane < hd
    q0 = jnp.where(head0, q, 0.0).astype(BF16)
    q1 = jnp.where(head0, 0.0, q).astype(BF16)
    bias0 = bias_ref[2 * p]
    bias1 = bias_ref[2 * p + 1]
    row = lax.broadcasted_iota(jnp.int32, (blk, blk), 0)
    col = lax.broadcasted_iota(jnp.int32, (blk, blk), 1)
    r2 = lax.broadcasted_iota(jnp.int32, (2 * blk, 2 * blk), 0)
    c2 = lax.broadcasted_iota(jnp.int32, (2 * blk, 2 * blk), 1)
    r2 = jnp.where(r2 >= blk, r2 - blk, r2)
    suffix = jnp.where((c2 >= blk) | (r2 > c2), 1.0, 0.0).astype(BF16)

    def body(i, carry):
        acc, carry0, carry1 = carry
        kb = qb - i
        start = pl.multiple_of(kb * blk, blk)
        kblk = k_ref[pl.ds(start, blk), :].astype(BF16)
        vblk = v_ref[pl.ds(start, blk), :]
        valid = (col + kb * blk) < (row + qb * blk)
        ws = []
        carries = []
        for qh, bias, car in ((q0, bias0, carry0), (q1, bias1, carry1)):
            z = lax.dot_general(qh, kblk, NT, preferred_element_type=F32) + bias
            ls_neg, ls_pos = _log_sigmoids(z)
            l1m = jnp.where(valid, ls_neg, 0.0)
            hi, lo = _split_bf16(l1m)
            cs = jnp.dot(jnp.concatenate([hi, lo], axis=1), suffix, preferred_element_type=F32)
            logw = ls_pos + car + cs[:, :blk]
            ws.append(jnp.where(valid, jnp.exp(logw), 0.0).astype(BF16))
            carries.append(car + cs[:, blk:])
        vbd = jnp.concatenate([jnp.where(head0, vblk, 0.0), jnp.where(head0, 0.0, vblk)], axis=0)
        acc = acc + jnp.dot(jnp.concatenate(ws, axis=1), vbd.astype(BF16), preferred_element_type=F32)
        return acc, carries[0], carries[1]

    zero = jnp.zeros((blk, LANES), F32)
    acc, _, _ = lax.fori_loop(0, qb + 1, body, (zero, zero, zero))
    o_ref[...] = acc


def _attn_prompt(q, k, v, bias, n_seq):
    n, d = q.shape
    t = n // n_seq
    blk = ATT_BLOCK
    n_blk = t // blk
    n_pairs = d // LANES
    return pl.pallas_call(
        functools.partial(_attn_prompt_kernel, blk=blk),
        grid_spec=pltpu.PrefetchScalarGridSpec(
            num_scalar_prefetch=1,
            grid=(n_seq, n_pairs, n_blk),
            in_specs=[
                pl.BlockSpec((blk, LANES), lambda bi, p, qb, bias_ref: (bi * n_blk + qb, p)),
                pl.BlockSpec((t, LANES), lambda bi, p, qb, bias_ref: (bi, p)),
                pl.BlockSpec((t, LANES), lambda bi, p, qb, bias_ref: (bi, p)),
            ],
            out_specs=pl.BlockSpec((blk, LANES), lambda bi, p, qb, bias_ref: (bi * n_blk + qb, p)),
        ),
        out_shape=jax.ShapeDtypeStruct((n, d), F32),
        compiler_params=_params(("parallel", "parallel", "arbitrary")),
        name="attn_prompt",
    )(bias, q, k, v)


def _attn_sample_kernel(pt_ref, q_ref, kn_ref, vn_ref, kc_ref, vc_ref, biasc_ref, o_ref,
                        qrep_scr, kpad_scr, vpad_scr, acc_scr, carry_scr, *, n_new, n_steps):
    j = pl.program_id(1)
    page = kc_ref.shape[1]
    d = q_ref.shape[2]
    n_heads = d // HEAD_DIM
    n_cols = n_heads * n_new
    rowc = lax.broadcasted_iota(jnp.int32, (LANES, d), 0)
    lanec = lax.broadcasted_iota(jnp.int32, (LANES, d), 1)
    own_head = (rowc < n_cols) & ((rowc // n_new) == (lanec // HEAD_DIM))

    def step(k_blk, v_blk, newest):
        zt = lax.dot_general(k_blk.astype(BF16), qrep_scr[...], NT, preferred_element_type=F32)
        zt = zt + biasc_ref[...]
        ls_neg, ls_pos = _log_sigmoids(zt)
        if newest:
            krow = lax.broadcasted_iota(jnp.int32, (page, LANES), 0)
            qcol = lax.broadcasted_iota(jnp.int32, (page, LANES), 1) % n_new
            valid = krow < qcol
            l1m = jnp.where(valid, ls_neg, 0.0)
        else:
            l1m = ls_neg
        hi, lo = _split_bf16(l1m)
        r2 = lax.broadcasted_iota(jnp.int32, (page, 2 * page), 0)
        c2 = lax.broadcasted_iota(jnp.int32, (page, 2 * page), 1)
        c2 = jnp.where(c2 >= page, c2 - page, c2)
        suffix = jnp.where(c2 >= r2, 1.0, 0.0).astype(BF16)
        cs = jnp.dot(suffix, jnp.concatenate([hi, lo], axis=0), preferred_element_type=F32)
        logw = ls_pos + carry_scr[...] + (cs - l1m)
        w = jnp.exp(logw)
        if newest:
            w = jnp.where(valid, w, 0.0)
        carry_scr[...] = carry_scr[...] + cs[0:1, :]
        wt = jnp.transpose(w).astype(BF16)
        acc_scr[...] += jnp.dot(wt, v_blk.astype(BF16), preferred_element_type=F32)

    @pl.when(j == 0)
    def _():
        rr = lax.broadcasted_iota(jnp.int32, (LANES, SAMPLE_PAD), 0)
        cc = lax.broadcasted_iota(jnp.int32, (LANES, SAMPLE_PAD), 1)
        rep = jnp.where((rr < n_cols) & ((rr % n_new) == cc), 1.0, 0.0).astype(BF16)
        qb = (q_ref[0] * SB_SCALE).astype(BF16)
        qrep = jnp.dot(rep, qb, preferred_element_type=F32)
        qrep_scr[...] = jnp.where(own_head, qrep, 0.0).astype(BF16)
        kpad_scr[...] = jnp.zeros_like(kpad_scr)
        vpad_scr[...] = jnp.zeros_like(vpad_scr)
        kpad_scr[0:SAMPLE_PAD, :] = kn_ref[0]
        vpad_scr[0:SAMPLE_PAD, :] = vn_ref[0]
        acc_scr[...] = jnp.zeros_like(acc_scr)
        carry_scr[...] = jnp.zeros_like(carry_scr)
        step(kpad_scr[...], vpad_scr[...], True)

    @pl.when(j > 0)
    def _():
        step(kc_ref[0], vc_ref[0], False)

    @pl.when(j == n_steps - 1)
    def _():
        rr = lax.broadcasted_iota(jnp.int32, (SAMPLE_PAD, LANES), 0)
        cc = lax.broadcasted_iota(jnp.int32, (SAMPLE_PAD, LANES), 1)
        pick = jnp.where((cc < n_cols) & ((cc % n_new) == rr), 1.0, 0.0)
        own = jnp.where(own_head, acc_scr[...], 0.0)
        o_ref[0] = jnp.dot(pick, own, preferred_element_type=F32, precision=HI)


def _attn_sample(q, k_new, v_new, cache_k, cache_v, page_table, bias_cols, n_new):
    n_seq, pad, d = q.shape
    n_pages = page_table.shape[1]
    page = cache_k.shape[1]
    n_steps = n_pages + 1

    def page_map(bi, j, pt_ref):
        return (pt_ref[bi, n_pages - jnp.maximum(j, 1)], 0, 0)

    row = pl.BlockSpec((1, pad, d), lambda bi, j, pt_ref: (bi, 0, 0))
    return pl.pallas_call(
        functools.partial(_attn_sample_kernel, n_new=n_new, n_steps=n_steps),
        grid_spec=pltpu.PrefetchScalarGridSpec(
            num_scalar_prefetch=1,
            grid=(n_seq, n_steps),
            in_specs=[
                row, row, row,
                pl.BlockSpec((1, page, d), page_map),
                pl.BlockSpec((1, page, d), page_map),
                pl.BlockSpec((1, LANES), lambda bi, j, pt_ref: (0, 0)),
            ],
            out_specs=row,
            scratch_shapes=[
                pltpu.VMEM((LANES, d), BF16),
                pltpu.VMEM((page, d), F32),
                pltpu.VMEM((page, d), F32),
                pltpu.VMEM((LANES, d), F32),
                pltpu.VMEM((1, LANES), F32),
            ],
        ),
        out_shape=jax.ShapeDtypeStruct((n_seq, pad, d), F32),
        compiler_params=_params(("parallel", "arbitrary")),
        name="attn_sample",
    )(page_table, q, k_new, v_new, cache_k, cache_v, bias_cols)


def kernel(x_prompt, x_sample, cache_k, cache_v, state_wkv, state_shift, page_table, ln_g, ln_b, ffn_w_gate, ffn_w_up, ffn_w_down, tm_mu, tm_w0, tm_w1, tm_w2, tm_a0, tm_a1, tm_a2, tm_v0, tm_v1, tm_v2, tm_g1, tm_g2, tm_k_k, tm_k_a, tm_r_k, tm_w_r, tm_w_k, tm_w_v, tm_w_o, tm_gn_g, tm_gn_b, sb_w_k, sb_w_v, sb_w_q, sb_bias, sb_w_o):
    bp, sp, d = x_prompt.shape
    bs, ss, _ = x_sample.shape
    depth = ln_g.shape[0]
    n_a = tm_mu.shape[0]
    n_heads = d // HEAD_DIM
    d_ff = ffn_w_gate.shape[-1]
    n_ff = d_ff // FFN_CHUNK
    n_p, n_s = bp * sp, bs * ss
    alpha = (2 * depth) ** 0.25

    def vec(a):
        return a.reshape(1, -1).astype(F32)

    hid = jnp.arange(d) // HEAD_DIM
    sel = (hid[:, None] == jnp.arange(LANES)[None, :]).astype(BF16)
    sel_t = sel.T

    x = jnp.concatenate([x_prompt.reshape(n_p, d), x_sample.reshape(n_s, d)], axis=0)

    def ffn(x, li, j):
        wg = ffn_w_gate[li, j].astype(BF16).reshape(d, n_ff, FFN_CHUNK).transpose(1, 0, 2)
        wu = ffn_w_up[li, j].astype(BF16).reshape(d, n_ff, FFN_CHUNK).transpose(1, 0, 2)
        wd = ffn_w_down[li, j].astype(BF16).reshape(n_ff, FFN_CHUNK, d)
        return _ffn_half(x, wg, wu, wd, vec(ln_g[li, 2 * j]), vec(ln_b[li, 2 * j]), alpha)

    new_wkv_p, new_wkv_s, new_shift_p, new_shift_s = [], [], [], []
    v_first = None
    k_sh = v_sh = None
    zero_state = jnp.zeros((bp, n_heads, HEAD_DIM, HEAD_DIM), F32)

    for li in range(depth):
        x = ffn(x, li, 0)
        if li < n_a:
            xp = x[:n_p].reshape(bp, sp, d)
            xs_ = x[n_p:].reshape(bs, ss, d)
            shift_p = jnp.concatenate([jnp.zeros((bp, 1, d), F32), xp[:, :-1]], axis=1)
            shift_s = jnp.concatenate([state_shift[li][:, None, :], xs_[:, :-1]], axis=1)
            x_shift = jnp.concatenate([shift_p.reshape(n_p, d), shift_s.reshape(n_s, d)], axis=0)
            new_shift_p.append(xp[:, -1])
            new_shift_s.append(xs_[:, -1])
            weights = [tm_mu[li], tm_w_r[li].astype(BF16), tm_w_k[li].astype(BF16), tm_w_v[li].astype(BF16),
                       vec(tm_w0[li]), tm_w1[li].astype(BF16), tm_w2[li].astype(BF16),
                       vec(tm_a0[li]), tm_a1[li].astype(BF16), tm_a2[li].astype(BF16)]
            if li > 0:
                weights += [vec(tm_v0[li - 1]), tm_v1[li - 1].astype(BF16), tm_v2[li - 1].astype(BF16)]
            weights += [tm_g1[li].astype(BF16), tm_g2[li].astype(BF16), vec(tm_k_k[li]), vec(tm_k_a[li]),
                        sel, sel_t]
            r, lw, k, v, a, b, g = _tm_pre(x, x_shift, v_first if li > 0 else None, weights)
            if li == 0:
                v_first = v
            y_p, s_p = _rwkv_scan(r[:n_p], lw[:n_p], k[:n_p], v[:n_p], a[:n_p], b[:n_p], zero_state,
                                  SCAN_CHUNK)

            def pad_rows(t):
                t = t[n_p:].reshape(bs, ss, d)
                return jnp.pad(t, ((0, 0), (0, SAMPLE_PAD - ss), (0, 0))).reshape(bs * SAMPLE_PAD, d)

            y_s, s_s = _rwkv_scan(*(pad_rows(t) for t in (r, lw, k, v, a, b)), state_wkv[li], SAMPLE_PAD)
            y_s = y_s.reshape(bs, SAMPLE_PAD, d)[:, :ss].reshape(n_s, d)
            new_wkv_p.append(s_p)
            new_wkv_s.append(s_s)
            y = jnp.concatenate([y_p, y_s], axis=0)
            post_w = [tm_w_o[li].astype(BF16), vec(tm_gn_g[li]), vec(tm_gn_b[li]), vec(tm_r_k[li]),
                      vec(ln_g[li, 1]), vec(ln_b[li, 1]), sel, sel_t]
            x = _tm_post(y, r, k, v, g, x, post_w, alpha)
        else:
            j = li - n_a
            (q,) = _proj(x, [sb_w_q[j].astype(BF16)], "q_proj")
            o_p = _attn_prompt(q[:n_p], k_sh[:n_p], v_sh[:n_p], sb_bias[j].astype(F32), bp)

            def pad_seq(t):
                t = t[n_p:].reshape(bs, ss, d)
                return jnp.pad(t, ((0, 0), (0, SAMPLE_PAD - ss), (0, 0)))

            bias_cols = jnp.pad(jnp.repeat(sb_bias[j].astype(F32), ss), (0, LANES - n_heads * ss))
            o_s = _attn_sample(pad_seq(q), pad_seq(k_sh), pad_seq(v_sh),
                               cache_k.reshape(cache_k.shape[0], cache_k.shape[1], d),
                               cache_v.reshape(cache_v.shape[0], cache_v.shape[1], d),
                               page_table, bias_cols.reshape(1, LANES), ss)
            o = jnp.concatenate([o_p, o_s[:, :ss].reshape(n_s, d)], axis=0)
            x = _oproj_postnorm(o, x, sb_w_o[j].astype(BF16), vec(ln_g[li, 1]), vec(ln_b[li, 1]), alpha)
        x = ffn(x, li, 1)
        if li == n_a - 1:
            k_sh, v_sh = _proj(x, [sb_w_k.astype(BF16), sb_w_v.astype(BF16)], "kv_proj")

    y_prompt = x[:n_p].reshape(bp, sp, d)
    y_sample = x[n_p:].reshape(bs, ss, d)
    k_prompt = k_sh[:n_p].reshape(bp, sp // PAGE_SIZE, PAGE_SIZE, n_heads, HEAD_DIM)
    v_prompt = v_sh[:n_p].reshape(bp, sp // PAGE_SIZE, PAGE_SIZE, n_heads, HEAD_DIM)
    k_sample = k_sh[n_p:].reshape(bs, ss, n_heads, HEAD_DIM)
    v_sample = v_sh[n_p:].reshape(bs, ss, n_heads, HEAD_DIM)
    return (y_prompt, y_sample, jnp.stack(new_wkv_p), jnp.stack(new_shift_p), k_prompt, v_prompt,
            jnp.stack(new_wkv_s), jnp.stack(new_shift_s), k_sample, v_sample)
```

```python
import functools

import jax
import jax.numpy as jnp
from jax import lax
from jax.experimental import pallas as pl
from jax.experimental.pallas import tpu as pltpu

HEAD_DIM = 64
LANES = 128
SUBLANES = 8
MXU_W = 256
LN_EPS = 1e-5
GN_EPS = 64e-5
SB_SCALE = HEAD_DIM ** -0.5
PAGE_SIZE = 128

FFN_CHUNK = MXU_W
TOKEN_TILE = 512
PRE_TILE = 256
SCAN_CHUNK = 64
SAMPLE_PAD = SUBLANES
ATT_Q = 256
ATT_K = LANES
ATT_SUB = 2
PAGES_PER_STEP = 4
VMEM_LIMIT = 56 * 1024 * 1024

F32 = jnp.float32
BF16 = jnp.bfloat16
HI = lax.Precision.HIGHEST
NT = (((1,), (1,)), ((), ()))


def _const_spec(shape):
    nd = len(shape)
    return pl.BlockSpec(shape, lambda *_: (0,) * nd, pipeline_mode=pl.Buffered(1))


def _params(sem):
    return pltpu.CompilerParams(dimension_semantics=sem, vmem_limit_bytes=VMEM_LIMIT)


def _iota(shape, axis):
    return lax.broadcasted_iota(jnp.int32, shape, axis)


def _layer_norm(y, g, b):
    mu = jnp.mean(y, axis=-1, keepdims=True)
    yc = y - mu
    var = jnp.mean(yc * yc, axis=-1, keepdims=True)
    return yc * lax.rsqrt(var + LN_EPS) * g + b


def _sigmoid(x):
    return 1.0 / (1.0 + jnp.exp(-x))


def _split_bf16(x):
    hi = x.astype(BF16)
    lo = (x - hi.astype(F32)).astype(BF16)
    return hi, lo


def _mm(a, b):
    return jnp.dot(a.astype(BF16), b.astype(BF16), preferred_element_type=F32)


def _mm_nt(a, b):
    return lax.dot_general(a.astype(BF16), b.astype(BF16), NT, preferred_element_type=F32)


def _head_sum(q, sel, sel_t):
    hi, lo = _split_bf16(q)
    s = jnp.dot(hi, sel, preferred_element_type=F32) + jnp.dot(lo, sel, preferred_element_type=F32)
    hi, lo = _split_bf16(s)
    return jnp.dot(hi, sel_t, preferred_element_type=F32) + jnp.dot(lo, sel_t, preferred_element_type=F32)


def _ffn_kernel(x_ref, wg_ref, wu_ref, wd_ref, g_ref, b_ref, o_ref, acc_ref, *, alpha, n_chunks):
    x = x_ref[...]
    xb = x.astype(BF16)
    acc_ref[...] = jnp.zeros_like(acc_ref)

    def body(c, carry):
        gate = jnp.dot(xb, wg_ref[c], preferred_element_type=F32)
        up = jnp.dot(xb, wu_ref[c], preferred_element_type=F32)
        h = (gate * _sigmoid(gate) * up).astype(BF16)
        acc_ref[...] += jnp.dot(h, wd_ref[c], preferred_element_type=F32)
        return carry

    lax.fori_loop(0, n_chunks, body, 0)
    y = alpha * x + 0.5 * acc_ref[...]
    o_ref[...] = _layer_norm(y, g_ref[...], b_ref[...])


def _ffn_half(x, wg, wu, wd, g, b, alpha):
    n, d = x.shape
    n_chunks = wg.shape[0]
    tm = TOKEN_TILE
    kern = functools.partial(_ffn_kernel, alpha=alpha, n_chunks=n_chunks)
    return pl.pallas_call(
        kern,
        grid=(n // tm,),
        in_specs=[
            pl.BlockSpec((tm, d), lambda i: (i, 0)),
            _const_spec(wg.shape), _const_spec(wu.shape), _const_spec(wd.shape),
            _const_spec(g.shape), _const_spec(b.shape),
        ],
        out_specs=pl.BlockSpec((tm, d), lambda i: (i, 0)),
        out_shape=jax.ShapeDtypeStruct((n, d), F32),
        scratch_shapes=[pltpu.VMEM((tm, d), F32)],
        compiler_params=_params(("parallel",)),
        name="ffn_half",
    )(x, wg, wu, wd, g, b)


def _tm_pre_kernel(*refs, has_vres, seq_len, tile):
    x_ref, shift_ref = refs[0], refs[1]
    refs = refs[2:]
    if has_vres:
        vf_ref, refs = refs[0], refs[1:]
    (mu_ref, wr_ref, wk_ref, wv_ref, w0_ref, w1_ref, w2_ref, a0_ref, a1_ref, a2_ref) = refs[:10]
    refs = refs[10:]
    if has_vres:
        (v0_ref, v1_ref, v2_ref), refs = refs[:3], refs[3:]
    (g1_ref, g2_ref, kk_ref, ka_ref, sel_ref, selt_ref,
     r_out, lw_out, k_out, v_out, a_out, b_out, g_out) = refs
    x = x_ref[...]
    if seq_len is None:
        xs = shift_ref[...]
    else:
        starts_sequence = (pl.program_id(0) * tile) % seq_len == 0
        first = jnp.where(starts_sequence, 0.0, shift_ref[SUBLANES - 1:SUBLANES, :])
        xs = jnp.where(_iota(x.shape, 0) == 0, first, pltpu.roll(x, 1, 0))
    xx = xs - x

    def mix(i):
        return (x + xx * mu_ref[i:i + 1, :]).astype(BF16)

    def mm(a, w_ref):
        return jnp.dot(a, w_ref[...], preferred_element_type=F32)

    xr, xw, xk, xv, xa, xg = (mix(i) for i in range(6))
    r_out[...] = mm(xr, wr_ref)
    w_in = w0_ref[...] + mm(jnp.tanh(mm(xw, w1_ref)).astype(BF16), w2_ref)
    softplus = jnp.maximum(-w_in, 0.0) + jnp.log1p(jnp.exp(-jnp.abs(w_in)))
    lw_out[...] = -jnp.exp(-softplus - 0.5)
    k = mm(xk, wk_ref)
    v = mm(xv, wv_ref)
    if has_vres:
        gate = _sigmoid(v0_ref[...] + mm(mm(xv, v1_ref).astype(BF16), v2_ref))
        v = v + (vf_ref[...] - v) * gate
    v_out[...] = v
    a = _sigmoid(a0_ref[...] + mm(mm(xa, a1_ref).astype(BF16), a2_ref))
    g_out[...] = mm(_sigmoid(mm(xg, g1_ref)).astype(BF16), g2_ref)
    kk = k * kk_ref[...]
    norm = jnp.sqrt(_head_sum(kk * kk, sel_ref[...], selt_ref[...]))
    kk = kk / jnp.maximum(norm, 1e-12)
    k_out[...] = k * (1.0 + (a - 1.0) * ka_ref[...])
    a_out[...] = -kk
    b_out[...] = kk * a


def _tm_pre(x, x_shift, v_first, weights, seq_len):
    n, d = x.shape
    tm = PRE_TILE
    has_vres = v_first is not None
    tile = pl.BlockSpec((tm, d), lambda i: (i, 0))
    if x_shift is None:
        per = tm // SUBLANES
        shift_in = x
        shift_spec = pl.BlockSpec((SUBLANES, d), lambda i: (jnp.maximum(i * per - 1, 0), 0))
    else:
        seq_len = None
        shift_in = x_shift
        shift_spec = tile
    acts = [x, shift_in] + ([v_first] if has_vres else [])
    kern = functools.partial(_tm_pre_kernel, has_vres=has_vres, seq_len=seq_len, tile=tm)
    out = jax.ShapeDtypeStruct((n, d), F32)
    return pl.pallas_call(
        kern,
        grid=(n // tm,),
        in_specs=[tile, shift_spec] + [tile] * has_vres + [_const_spec(w.shape) for w in weights],
        out_specs=[tile] * 7,
        out_shape=[out] * 7,
        compiler_params=_params(("parallel",)),
        name="tm_pre_vres" if has_vres else "tm_pre",
    )(*acts, *weights)


def _scan_kernel(r_ref, lw_ref, k_ref, v_ref, a_ref, b_ref, s0_ref, y_ref, sout_ref, s_scr,
                 *, chunk, n_chunks, n_groups):
    c = pl.program_id(1)
    hd = HEAD_DIM
    gw = MXU_W
    hpg = gw // hd
    pw = hpg * chunk

    @pl.when(c == 0)
    def _():
        s_scr[...] = jnp.zeros_like(s_scr)
        for g in range(n_groups):
            for h in range(hpg):
                s_scr[g, h * hd:(h + 1) * hd, h * hd:(h + 1) * hd] = s0_ref[0, g * hpg + h]

    row_p = _iota((chunk, pw), 0)
    j_p = _iota((chunk, pw), 1) % chunk
    strict = j_p < row_p
    incl = j_p <= row_p
    tri = jnp.where(_iota((chunk, chunk), 1) <= _iota((chunk, chunk), 0), 1.0, 0.0).astype(BF16)
    bd_pp = (_iota((pw, pw), 0) // chunk) == (_iota((pw, pw), 1) // chunk)
    bd_pg = (_iota((pw, gw), 0) // chunk) == (_iota((pw, gw), 1) // hd)
    bd_gg = (_iota((gw, gw), 0) // hd) == (_iota((gw, gw), 1) // hd)

    def heads_on_rows(x, mask):
        return jnp.where(mask, jnp.concatenate([x] * hpg, axis=0), 0.0).astype(BF16)

    groups = range(n_groups)
    sls = [slice(g * gw, (g + 1) * gw) for g in groups]
    vs, kbs, w_lasts, ars, r_bs, r_ks = [], [], [], [], [], []
    for sl in sls:
        r, lw, k, v, a, b = (ref[:, sl] for ref in (r_ref, lw_ref, k_ref, v_ref, a_ref, b_ref))
        hi = lw.astype(BF16)
        rem = lw - hi.astype(F32)
        mid = rem.astype(BF16)
        lo = (rem - mid.astype(F32)).astype(BF16)
        cum = (jnp.dot(tri, hi, preferred_element_type=F32) + jnp.dot(tri, mid, preferred_element_type=F32)
               + jnp.dot(tri, lo, preferred_element_type=F32))
        w_last = cum[chunk - 1:chunk, :]
        w_neg = jnp.exp(-cum)
        w_rest = jnp.exp(w_last - cum)
        vs.append(v)
        w_lasts.append(w_last)
        kbs.append(jnp.concatenate([k * w_rest, b * w_rest], axis=0))
        ars.append(jnp.concatenate([a * jnp.exp(cum - lw), r * jnp.exp(cum)], axis=0).astype(BF16))
        r_bs.append(heads_on_rows(b * w_neg, bd_pg))
        r_ks.append(heads_on_rows(k * w_neg, bd_pg))

    ss = [s_scr[g] for g in groups]
    g_bs = [_mm_nt(ars[g], r_bs[g]) for g in groups]
    g_ks = [_mm_nt(ars[g], r_ks[g]) for g in groups]
    from_state = [_mm_nt(ars[g], ss[g]) for g in groups]
    v_bds = [heads_on_rows(vs[g], bd_pg) for g in groups]
    a_abs = [jnp.where(strict, g_bs[g][:chunk], 0.0) for g in groups]
    a_rbs = [jnp.where(incl, g_bs[g][chunk:], 0.0) for g in groups]
    a_aks = [jnp.where(strict, g_ks[g][:chunk], 0.0) for g in groups]
    a_rks = [jnp.where(incl, g_ks[g][chunk:], 0.0) for g in groups]
    us = [from_state[g][:chunk] + _mm(a_aks[g], v_bds[g]) for g in groups]
    powers = a_abs
    span = 1
    while True:
        us = [us[g] + _mm(powers[g], heads_on_rows(us[g], bd_pg)) for g in groups]
        span *= 2
        if span >= chunk:
            break
        powers = [_mm(powers[g], heads_on_rows(powers[g], bd_pp)) for g in groups]
    for g in groups:
        y_ref[:, sls[g]] = (from_state[g][chunk:] + _mm(a_rks[g], v_bds[g])
                            + _mm(a_rbs[g], heads_on_rows(us[g], bd_pg)))
    for g in groups:
        vu_t = jnp.transpose(jnp.concatenate([vs[g], us[g]], axis=0))
        s_scr[g] = ss[g] * jnp.exp(w_lasts[g]) + jnp.where(bd_gg, _mm(vu_t, kbs[g]), 0.0)

    @pl.when(c == n_chunks - 1)
    def _():
        for g in range(n_groups):
            for h in range(hpg):
                sout_ref[0, g * hpg + h] = s_scr[g, h * hd:(h + 1) * hd, h * hd:(h + 1) * hd]


def _rwkv_scan(r, lw, k, v, a, b, s0, chunk):
    n, d = r.shape
    n_seq, n_heads = s0.shape[0], s0.shape[1]
    t = n // n_seq
    n_chunks = t // chunk
    n_groups = d // MXU_W
    tile = pl.BlockSpec((chunk, d), lambda bi, c: (bi * n_chunks + c, 0))
    st = pl.BlockSpec((1, n_heads, HEAD_DIM, HEAD_DIM), lambda bi, c: (bi, 0, 0, 0))
    kern = functools.partial(_scan_kernel, chunk=chunk, n_chunks=n_chunks, n_groups=n_groups)
    return pl.pallas_call(
        kern,
        grid=(n_seq, n_chunks),
        in_specs=[tile] * 6 + [st],
        out_specs=[tile, st],
        out_shape=[jax.ShapeDtypeStruct((n, d), F32), jax.ShapeDtypeStruct(s0.shape, F32)],
        scratch_shapes=[pltpu.VMEM((n_groups, MXU_W, MXU_W), F32)],
        compiler_params=_params(("parallel", "arbitrary")),
        name=f"rwkv_scan_c{chunk}",
    )(r, lw, k, v, a, b, s0)


def _tm_post_kernel(y_ref, r_ref, k_ref, v_ref, g_ref, x_ref, wo_ref, gng_ref, gnb_ref, rk_ref,
                    lng_ref, lnb_ref, sel_ref, selt_ref, o_ref, *, alpha):
    sel, selt = sel_ref[...], selt_ref[...]
    y = y_ref[...]
    inv = 1.0 / HEAD_DIM
    yc = y - _head_sum(y, sel, selt) * inv
    var = _head_sum(yc * yc, sel, selt) * inv
    yn = yc * lax.rsqrt(var + GN_EPS) * gng_ref[...] + gnb_ref[...]
    bonus = _head_sum(r_ref[...] * k_ref[...] * rk_ref[...], sel, selt) * v_ref[...]
    z = ((yn + bonus) * g_ref[...]).astype(BF16)
    out = jnp.dot(z, wo_ref[...], preferred_element_type=F32)
    o_ref[...] = _layer_norm(alpha * x_ref[...] + out, lng_ref[...], lnb_ref[...])


def _tm_post(y, r, k, v, g, x, weights, alpha):
    n, d = x.shape
    tm = PRE_TILE
    tile = pl.BlockSpec((tm, d), lambda i: (i, 0))
    return pl.pallas_call(
        functools.partial(_tm_post_kernel, alpha=alpha),
        grid=(n // tm,),
        in_specs=[tile] * 6 + [_const_spec(w.shape) for w in weights],
        out_specs=tile,
        out_shape=jax.ShapeDtypeStruct((n, d), F32),
        compiler_params=_params(("parallel",)),
        name="tm_post",
    )(y, r, k, v, g, x, *weights)


def _proj_kernel(x_ref, *refs):
    n_out = len(refs) // 2
    xb = x_ref[...].astype(BF16)
    for w_ref, o_ref in zip(refs[:n_out], refs[n_out:]):
        o_ref[...] = jnp.dot(xb, w_ref[...], preferred_element_type=F32)


def _proj(x, ws, name):
    n, d = x.shape
    tm = TOKEN_TILE
    tile = pl.BlockSpec((tm, d), lambda i: (i, 0))
    return pl.pallas_call(
        _proj_kernel,
        grid=(n // tm,),
        in_specs=[tile] + [_const_spec(w.shape) for w in ws],
        out_specs=[pl.BlockSpec((tm, w.shape[1]), lambda i: (i, 0)) for w in ws],
        out_shape=[jax.ShapeDtypeStruct((n, w.shape[1]), F32) for w in ws],
        compiler_params=_params(("parallel",)),
        name=name,
    )(x, *ws)


def _oproj_kernel(o_ref, x_ref, w_ref, g_ref, b_ref, out_ref, *, alpha):
    sub = jnp.dot(o_ref[...].astype(BF16), w_ref[...], preferred_element_type=F32)
    out_ref[...] = _layer_norm(alpha * x_ref[...] + sub, g_ref[...], b_ref[...])


def _oproj_postnorm(o, x, w, g, b, alpha):
    n, d = x.shape
    tm = TOKEN_TILE
    tile = pl.BlockSpec((tm, d), lambda i: (i, 0))
    return pl.pallas_call(
        functools.partial(_oproj_kernel, alpha=alpha),
        grid=(n // tm,),
        in_specs=[tile, tile, _const_spec(w.shape), _const_spec(g.shape), _const_spec(b.shape)],
        out_specs=tile,
        out_shape=jax.ShapeDtypeStruct((n, d), F32),
        compiler_params=_params(("parallel",)),
        name="oproj_postnorm",
    )(o, x, w, g, b)


LOG2E = 1.4426950408889634


def _log2_survive(nz2):
    sign = jnp.uint32(0x80000000)
    neg_abs = lax.bitcast_convert_type(lax.bitcast_convert_type(nz2, jnp.uint32) | sign, F32)
    return jnp.minimum(nz2, 0.0) - jnp.log2(1.0 + jnp.exp2(neg_abs))


def _attn_prompt_kernel(bias_ref, q_ref, k_ref, v_ref, o_ref, *, tq, tk, n_sub):
    p = pl.program_id(1)
    qb = pl.program_id(2)
    hd = HEAD_DIM
    q = q_ref[...] * (-SB_SCALE * LOG2E)
    head0 = _iota((tq, LANES), 1) < hd
    q0 = jnp.where(head0, q, 0.0).astype(BF16)
    q1 = jnp.where(head0, 0.0, q).astype(BF16)
    bias0 = bias_ref[2 * p] * (-LOG2E)
    bias1 = bias_ref[2 * p + 1] * (-LOG2E)
    row = _iota((tq, tk), 0)
    col = _iota((tq, tk), 1)
    head0_k = _iota((tk, LANES), 1) < hd
    r2 = _iota((tk, 2 * tk), 0)
    c2 = _iota((tk, 2 * tk), 1)
    suffix = jnp.where((c2 >= tk) | (r2 > c2), 1.0, 0.0).astype(BF16)

    kpi = n_sub * tk
    heads = ((q0, bias0), (q1, bias1))
    diag = (qb * tq) // kpi
    shift = diag * kpi - qb * tq

    def key_blocks(start, carry, diagonal):
        acc, carry0, carry1 = carry
        subs = list(reversed(range(n_sub)))
        valid = [(col + (sub * tk + shift)) < row if diagonal else None for sub in subs]
        kblks = [k_ref[pl.ds(start + sub * tk, tk), :].astype(BF16) for sub in subs]
        nzs = [[lax.dot_general(qh, kblk, NT, preferred_element_type=F32) + bias for qh, bias in heads]
               for kblk in kblks]
        l1m = [[_log2_survive(nz) for nz in nzh] for nzh in nzs]
        if diagonal:
            l1m = [[jnp.where(valid[i], l, 0.0) for l in l1m[i]] for i in range(n_sub)]
        css = [[jnp.dot(l.astype(BF16), suffix, preferred_element_type=F32) for l in lh] for lh in l1m]
        cars = [carry0, carry1]
        ws = []
        for i in range(n_sub):
            wh = []
            for h in range(2):
                w = jnp.exp2(l1m[i][h] - nzs[i][h] + cars[h] + css[i][h][:, :tk])
                if diagonal:
                    w = jnp.where(valid[i], w, 0.0)
                wh.append(w.astype(BF16))
                cars[h] = cars[h] + css[i][h][:, tk:]
            ws.append(jnp.concatenate(wh, axis=1))
        for i, sub in enumerate(subs):
            vblk = v_ref[pl.ds(start + sub * tk, tk), :]
            vbd = jnp.concatenate([jnp.where(head0_k, vblk, 0.0), jnp.where(head0_k, 0.0, vblk)], axis=0)
            acc = acc + jnp.dot(ws[i], vbd.astype(BF16), preferred_element_type=F32)
        return acc, cars[0], cars[1]

    zero = jnp.zeros((tq, LANES), F32)
    carry = key_blocks(pl.multiple_of(diag * kpi, kpi), (zero, zero, zero), True)

    def body(i, carry):
        return key_blocks(pl.multiple_of((diag - i) * kpi, kpi), carry, False)

    acc, _, _ = lax.fori_loop(1, diag + 1, body, carry)
    o_ref[...] = acc


def _attn_prompt(q, k, v, bias, n_seq):
    n, d = q.shape
    t = n // n_seq
    tq, tk, n_sub = ATT_Q, ATT_K, ATT_SUB
    assert tk == LANES and t % tq == 0 and t % (n_sub * tk) == 0
    n_blk = t // tq
    n_pairs = d // LANES
    return pl.pallas_call(
        functools.partial(_attn_prompt_kernel, tq=tq, tk=tk, n_sub=n_sub),
        grid_spec=pltpu.PrefetchScalarGridSpec(
            num_scalar_prefetch=1,
            grid=(n_seq, n_pairs, n_blk),
            in_specs=[
                pl.BlockSpec((tq, LANES), lambda bi, p, qb, bias_ref: (bi * n_blk + qb, p)),
                pl.BlockSpec((t, LANES), lambda bi, p, qb, bias_ref: (bi, p)),
                pl.BlockSpec((t, LANES), lambda bi, p, qb, bias_ref: (bi, p)),
            ],
            out_specs=pl.BlockSpec((tq, LANES), lambda bi, p, qb, bias_ref: (bi * n_blk + qb, p)),
        ),
        out_shape=jax.ShapeDtypeStruct((n, d), F32),
        compiler_params=_params(("parallel", "parallel", "arbitrary")),
        name="attn_prompt",
    )(bias, q, k, v)


def _attn_sample_kernel(pt_ref, q_ref, kn_ref, vn_ref, *refs, n_new, n_steps, n_pg):
    kc_refs, vc_refs = refs[:n_pg], refs[n_pg:2 * n_pg]
    biasc_ref, o_ref, qrep_scr, kpad_scr, vpad_scr, acc_scr, carry_scr = refs[2 * n_pg:]
    j = pl.program_id(1)
    page = kc_refs[0].shape[1]
    d = q_ref.shape[2]
    n_rows = (d // HEAD_DIM) * n_new
    assert page == LANES
    own_head = (_iota((n_rows, d), 0) // n_new) == (_iota((n_rows, d), 1) // HEAD_DIM)
    r2 = _iota((page, 2 * page), 0)
    c2 = _iota((page, 2 * page), 1)
    suffix = jnp.where((c2 >= page) | (r2 > c2), 1.0, 0.0).astype(BF16)

    def visit(k_cat, v_cat, n_blk, valid):
        bias = jnp.concatenate([biasc_ref[...]] * n_blk, axis=1)
        nz = lax.dot_general(qrep_scr[...], k_cat, NT, preferred_element_type=F32) + bias
        l1m = _log2_survive(nz)
        if valid is not None:
            l1m = jnp.where(valid, l1m, 0.0)
        blks = [slice(i * page, (i + 1) * page) for i in range(n_blk)]
        cs = [jnp.dot(l1m[:, sl].astype(BF16), suffix, preferred_element_type=F32) for sl in blks]
        carry = carry_scr[...]
        ws = [None] * n_blk
        for i in reversed(range(n_blk)):
            ws[i] = jnp.exp2(l1m[:, blks[i]] - nz[:, blks[i]] + cs[i][:, :page] + carry)
            carry = carry + cs[i][:, page:]
        carry_scr[...] = carry
        w = jnp.concatenate(ws, axis=1)
        if valid is not None:
            w = jnp.where(valid, w, 0.0)
        acc_scr[...] += jnp.dot(w.astype(BF16), v_cat, preferred_element_type=F32)

    @pl.when(j == 0)
    def _():
        rr = _iota((n_rows, SAMPLE_PAD), 0)
        cc = _iota((n_rows, SAMPLE_PAD), 1)
        rep = jnp.where((rr % n_new) == cc, 1.0, 0.0).astype(BF16)
        qrep = jnp.dot(rep, (q_ref[0] * (-SB_SCALE * LOG2E)).astype(BF16), preferred_element_type=F32)
        qrep_scr[...] = jnp.where(own_head, qrep, 0.0).astype(BF16)
        kpad_scr[...] = jnp.zeros_like(kpad_scr)
        vpad_scr[...] = jnp.zeros_like(vpad_scr)
        kpad_scr[0:SAMPLE_PAD, :] = kn_ref[0].astype(BF16)
        vpad_scr[0:SAMPLE_PAD, :] = vn_ref[0].astype(BF16)
        acc_scr[...] = jnp.zeros_like(acc_scr)
        carry_scr[...] = jnp.zeros_like(carry_scr)
        valid = _iota((n_rows, page), 1) < (_iota((n_rows, page), 0) % n_new)
        visit(kpad_scr[...], vpad_scr[...], 1, valid)

    @pl.when(j > 0)
    def _():
        k_cat = jnp.concatenate([kc_refs[i][0] for i in range(n_pg)], axis=0)
        v_cat = jnp.concatenate([vc_refs[i][0] for i in range(n_pg)], axis=0)
        visit(k_cat, v_cat, n_pg, None)

    @pl.when(j == n_steps - 1)
    def _():
        rr = _iota((SAMPLE_PAD, n_rows), 0)
        cc = _iota((SAMPLE_PAD, n_rows), 1)
        pick = jnp.where((cc % n_new) == rr, 1.0, 0.0)
        own = jnp.where(own_head, acc_scr[...], 0.0)
        o_ref[0] = jnp.dot(pick, own, preferred_element_type=F32, precision=HI)


def _attn_sample(q, k_new, v_new, cache_k, cache_v, page_table, bias_rows, n_new):
    n_seq, pad, d = q.shape
    n_pages = page_table.shape[1]
    page = cache_k.shape[1]
    n_pg = PAGES_PER_STEP
    assert n_pages % n_pg == 0
    n_steps = n_pages // n_pg + 1

    def page_spec(i):
        def index(bi, j, pt_ref):
            return (pt_ref[bi, n_pages - n_pg * jnp.maximum(j, 1) + i], 0, 0)
        return pl.BlockSpec((1, page, d), index)

    row = pl.BlockSpec((1, pad, d), lambda bi, j, pt_ref: (bi, 0, 0))
    pages = [page_spec(i) for i in range(n_pg)]
    n_rows = bias_rows.shape[0]
    return pl.pallas_call(
        functools.partial(_attn_sample_kernel, n_new=n_new, n_steps=n_steps, n_pg=n_pg),
        grid_spec=pltpu.PrefetchScalarGridSpec(
            num_scalar_prefetch=1,
            grid=(n_seq, n_steps),
            in_specs=[row, row, row] + pages + pages
                     + [pl.BlockSpec((n_rows, LANES), lambda bi, j, pt_ref: (0, 0))],
            out_specs=row,
            scratch_shapes=[
                pltpu.VMEM((n_rows, d), BF16),
                pltpu.VMEM((page, d), BF16),
                pltpu.VMEM((page, d), BF16),
                pltpu.VMEM((n_rows, d), F32),
                pltpu.VMEM((n_rows, LANES), F32),
            ],
        ),
        out_shape=jax.ShapeDtypeStruct((n_seq, pad, d), F32),
        compiler_params=_params(("parallel", "arbitrary")),
        name="attn_sample",
    )(page_table, q, k_new, v_new, *([cache_k] * n_pg), *([cache_v] * n_pg), bias_rows)


def kernel(x_prompt, x_sample, cache_k, cache_v, state_wkv, state_shift, page_table, ln_g, ln_b, ffn_w_gate, ffn_w_up, ffn_w_down, tm_mu, tm_w0, tm_w1, tm_w2, tm_a0, tm_a1, tm_a2, tm_v0, tm_v1, tm_v2, tm_g1, tm_g2, tm_k_k, tm_k_a, tm_r_k, tm_w_r, tm_w_k, tm_w_v, tm_w_o, tm_gn_g, tm_gn_b, sb_w_k, sb_w_v, sb_w_q, sb_bias, sb_w_o):
    bp, sp, d = x_prompt.shape
    bs, ss, _ = x_sample.shape
    depth = ln_g.shape[0]
    n_a = tm_mu.shape[0]
    n_heads = d // HEAD_DIM
    n_ff = ffn_w_gate.shape[-1] // FFN_CHUNK
    alpha = (2 * depth) ** 0.25

    def vec(a):
        return a.reshape(1, -1).astype(F32)

    def bf(a):
        return a.astype(BF16)

    hid = jnp.arange(d) // HEAD_DIM
    sel = (hid[:, None] == jnp.arange(LANES)[None, :]).astype(BF16)
    sel_t = sel.T

    ffn_w = {}
    for li in range(depth):
        for j in range(2):
            ffn_w[li, j] = (
                bf(ffn_w_gate[li, j]).reshape(d, n_ff, FFN_CHUNK).transpose(1, 0, 2),
                bf(ffn_w_up[li, j]).reshape(d, n_ff, FFN_CHUNK).transpose(1, 0, 2),
                bf(ffn_w_down[li, j]).reshape(n_ff, FFN_CHUNK, d),
                vec(ln_g[li, 2 * j]), vec(ln_b[li, 2 * j]))
    pre_w, post_w = [], []
    for li in range(n_a):
        w = [tm_mu[li], bf(tm_w_r[li]), bf(tm_w_k[li]), bf(tm_w_v[li]),
             vec(tm_w0[li]), bf(tm_w1[li]), bf(tm_w2[li]), vec(tm_a0[li]), bf(tm_a1[li]), bf(tm_a2[li])]
        if li > 0:
            w += [vec(tm_v0[li - 1]), bf(tm_v1[li - 1]), bf(tm_v2[li - 1])]
        w += [bf(tm_g1[li]), bf(tm_g2[li]), vec(tm_k_k[li]), vec(tm_k_a[li]), sel, sel_t]
        pre_w.append(w)
        post_w.append([bf(tm_w_o[li]), vec(tm_gn_g[li]), vec(tm_gn_b[li]), vec(tm_r_k[li]),
                       vec(ln_g[li, 1]), vec(ln_b[li, 1]), sel, sel_t])
    wq = [bf(sb_w_q[j]) for j in range(depth - n_a)]
    wo = [bf(sb_w_o[j]) for j in range(depth - n_a)]
    wkv = [bf(sb_w_k), bf(sb_w_v)]
    n_phys, page = cache_k.shape[0], cache_k.shape[1]
    cache_k2 = bf(cache_k.reshape(n_phys, page, d))
    cache_v2 = bf(cache_v.reshape(n_phys, page, d))

    def trunk(x, n_seq, t, wkv_in, shift_in):
        fresh = shift_in is None
        n = n_seq * t
        new_wkv, new_shift = [], []
        v_first = None
        k_sh = v_sh = None

        def pad_rows(a):
            a = a.reshape(n_seq, t, d)
            return jnp.pad(a, ((0, 0), (0, SAMPLE_PAD - t), (0, 0)))

        for li in range(depth):
            x = _ffn_half(x, *ffn_w[li, 0], alpha)
            if li < n_a:
                x3 = x.reshape(n_seq, t, d)
                new_shift.append(x3[:, -1])
                if fresh:
                    x_shift = None
                else:
                    x_shift = jnp.concatenate([shift_in[li][:, None, :], x3[:, :-1]], axis=1).reshape(n, d)
                r, lw, k, v, a, b, g = _tm_pre(x, x_shift, v_first if li > 0 else None, pre_w[li], t)
                if li == 0:
                    v_first = v
                if fresh:
                    y, s_new = _rwkv_scan(r, lw, k, v, a, b, wkv_in[li], SCAN_CHUNK)
                else:
                    padded = (pad_rows(z).reshape(n_seq * SAMPLE_PAD, d) for z in (r, lw, k, v, a, b))
                    y, s_new = _rwkv_scan(*padded, wkv_in[li], SAMPLE_PAD)
                    y = y.reshape(n_seq, SAMPLE_PAD, d)[:, :t].reshape(n, d)
                new_wkv.append(s_new)
                x = _tm_post(y, r, k, v, g, x, post_w[li], alpha)
            else:
                j = li - n_a
                (q,) = _proj(x, [wq[j]], "q_proj")
                if fresh:
                    o = _attn_prompt(q, k_sh, v_sh, sb_bias[j].astype(F32), n_seq)
                else:
                    nbias2 = sb_bias[j].astype(F32) * (-LOG2E)
                    bias_rows = jnp.broadcast_to(jnp.repeat(nbias2, t)[:, None], (n_heads * t, LANES))
                    o = _attn_sample(pad_rows(q), pad_rows(k_sh), pad_rows(v_sh), cache_k2, cache_v2,
                                     page_table, bias_rows, t)
                    o = o[:, :t].reshape(n, d)
                x = _oproj_postnorm(o, x, wo[j], vec(ln_g[li, 1]), vec(ln_b[li, 1]), alpha)
            x = _ffn_half(x, *ffn_w[li, 1], alpha)
            if li == n_a - 1:
                k_sh, v_sh = _proj(x, wkv, "kv_proj")
        return x, jnp.stack(new_wkv), jnp.stack(new_shift), k_sh, v_sh

    zero_state = jnp.zeros((n_a, bp, n_heads, HEAD_DIM, HEAD_DIM), F32)
    y_p, wkv_p, shift_p, k_p, v_p = trunk(x_prompt.reshape(bp * sp, d), bp, sp, zero_state, None)
    y_s, wkv_s, shift_s, k_s, v_s = trunk(x_sample.reshape(bs * ss, d), bs, ss, state_wkv, state_shift)

    def paged(a):
        return a.reshape(bp, sp // PAGE_SIZE, PAGE_SIZE, n_heads, HEAD_DIM)

    return (y_p.reshape(bp, sp, d), y_s.reshape(bs, ss, d), wkv_p, shift_p, paged(k_p), paged(v_p),
            wkv_s, shift_s, k_s.reshape(bs, ss, n_heads, HEAD_DIM), v_s.reshape(bs, ss, n_heads, HEAD_DIM))
```

```python
import functools

import jax
import jax.numpy as jnp
from jax import lax
from jax.experimental import pallas as pl
from jax.experimental.pallas import tpu as pltpu

HEAD_DIM = 64
LANES = 128
SUBLANES = 8
MXU_W = 256
LN_EPS = 1e-5
GN_EPS = 64e-5
SB_SCALE = HEAD_DIM ** -0.5
PAGE_SIZE = 128

FFN_CHUNK = MXU_W
TOKEN_TILE = 512
PRE_TILE = 256
SCAN_CHUNK = 64
SAMPLE_PAD = SUBLANES
ATT_Q = 256
ATT_K = LANES
ATT_SUB = 4
PAGES_PER_STEP = 8
VMEM_LIMIT = 56 * 1024 * 1024

F32 = jnp.float32
BF16 = jnp.bfloat16
HI = lax.Precision.HIGHEST
NT = (((1,), (1,)), ((), ()))


def _const_spec(shape):
    nd = len(shape)
    return pl.BlockSpec(shape, lambda *_: (0,) * nd, pipeline_mode=pl.Buffered(1))


def _params(sem):
    return pltpu.CompilerParams(dimension_semantics=sem, vmem_limit_bytes=VMEM_LIMIT)


def _iota(shape, axis):
    return lax.broadcasted_iota(jnp.int32, shape, axis)


def _layer_norm(y, g, b):
    mu = jnp.mean(y, axis=-1, keepdims=True)
    yc = y - mu
    var = jnp.mean(yc * yc, axis=-1, keepdims=True)
    return yc * lax.rsqrt(var + LN_EPS) * g + b


def _sigmoid(x):
    return 1.0 / (1.0 + jnp.exp(-x))


def _split_bf16(x):
    hi = x.astype(BF16)
    lo = (x - hi.astype(F32)).astype(BF16)
    return hi, lo


def _mm(a, b):
    return jnp.dot(a.astype(BF16), b.astype(BF16), preferred_element_type=F32)


def _mm_nt(a, b):
    return lax.dot_general(a.astype(BF16), b.astype(BF16), NT, preferred_element_type=F32)


def _head_sum(q, sel, sel_t):
    hi, lo = _split_bf16(q)
    s = jnp.dot(hi, sel, preferred_element_type=F32) + jnp.dot(lo, sel, preferred_element_type=F32)
    hi, lo = _split_bf16(s)
    return jnp.dot(hi, sel_t, preferred_element_type=F32) + jnp.dot(lo, sel_t, preferred_element_type=F32)


def _ffn_kernel(x_ref, wg_ref, wu_ref, wd_ref, g_ref, b_ref, o_ref, acc_ref, *, alpha, n_chunks):
    x = x_ref[...]
    xb = x.astype(BF16)
    acc_ref[...] = jnp.zeros_like(acc_ref)

    def body(c, carry):
        gate = jnp.dot(xb, wg_ref[c], preferred_element_type=F32)
        up = jnp.dot(xb, wu_ref[c], preferred_element_type=F32)
        h = (gate * _sigmoid(gate) * up).astype(BF16)
        acc_ref[...] += jnp.dot(h, wd_ref[c], preferred_element_type=F32)
        return carry

    lax.fori_loop(0, n_chunks, body, 0)
    y = alpha * x + 0.5 * acc_ref[...]
    o_ref[...] = _layer_norm(y, g_ref[...], b_ref[...])


def _ffn_half(x, wg, wu, wd, g, b, alpha):
    n, d = x.shape
    n_chunks = wg.shape[0]
    tm = TOKEN_TILE
    kern = functools.partial(_ffn_kernel, alpha=alpha, n_chunks=n_chunks)
    return pl.pallas_call(
        kern,
        grid=(n // tm,),
        in_specs=[
            pl.BlockSpec((tm, d), lambda i: (i, 0)),
            _const_spec(wg.shape), _const_spec(wu.shape), _const_spec(wd.shape),
            _const_spec(g.shape), _const_spec(b.shape),
        ],
        out_specs=pl.BlockSpec((tm, d), lambda i: (i, 0)),
        out_shape=jax.ShapeDtypeStruct((n, d), F32),
        scratch_shapes=[pltpu.VMEM((tm, d), F32)],
        compiler_params=_params(("parallel",)),
        name="ffn_half",
    )(x, wg, wu, wd, g, b)


def _tm_pre_kernel(*refs, has_vres, seq_len, tile):
    x_ref, shift_ref = refs[0], refs[1]
    refs = refs[2:]
    if has_vres:
        vf_ref, refs = refs[0], refs[1:]
    (mu_ref, wr_ref, wk_ref, wv_ref, w0_ref, w1_ref, w2_ref, a0_ref, a1_ref, a2_ref) = refs[:10]
    refs = refs[10:]
    if has_vres:
        (v0_ref, v1_ref, v2_ref), refs = refs[:3], refs[3:]
    (g1_ref, g2_ref, kk_ref, ka_ref, sel_ref, selt_ref,
     r_out, lw_out, k_out, v_out, a_out, b_out, g_out) = refs
    x = x_ref[...]
    if seq_len is None:
        xs = shift_ref[...]
    else:
        starts_sequence = (pl.program_id(0) * tile) % seq_len == 0
        first = jnp.where(starts_sequence, 0.0, shift_ref[SUBLANES - 1:SUBLANES, :])
        xs = jnp.where(_iota(x.shape, 0) == 0, first, pltpu.roll(x, 1, 0))
    xx = xs - x

    def mix(i):
        return (x + xx * mu_ref[i:i + 1, :]).astype(BF16)

    def mm(a, w_ref):
        return jnp.dot(a, w_ref[...], preferred_element_type=F32)

    xr, xw, xk, xv, xa, xg = (mix(i) for i in range(6))
    r_out[...] = mm(xr, wr_ref)
    w_in = w0_ref[...] + mm(jnp.tanh(mm(xw, w1_ref)).astype(BF16), w2_ref)
    softplus = jnp.maximum(-w_in, 0.0) + jnp.log1p(jnp.exp(-jnp.abs(w_in)))
    lw_out[...] = -jnp.exp(-softplus - 0.5)
    k = mm(xk, wk_ref)
    v = mm(xv, wv_ref)
    if has_vres:
        gate = _sigmoid(v0_ref[...] + mm(mm(xv, v1_ref).astype(BF16), v2_ref))
        v = v + (vf_ref[...] - v) * gate
    v_out[...] = v
    a = _sigmoid(a0_ref[...] + mm(mm(xa, a1_ref).astype(BF16), a2_ref))
    g_out[...] = mm(_sigmoid(mm(xg, g1_ref)).astype(BF16), g2_ref)
    kk = k * kk_ref[...]
    norm = jnp.sqrt(_head_sum(kk * kk, sel_ref[...], selt_ref[...]))
    kk = kk / jnp.maximum(norm, 1e-12)
    k_out[...] = k * (1.0 + (a - 1.0) * ka_ref[...])
    a_out[...] = -kk
    b_out[...] = kk * a


def _tm_pre(x, x_shift, v_first, weights, seq_len):
    n, d = x.shape
    tm = PRE_TILE
    has_vres = v_first is not None
    tile = pl.BlockSpec((tm, d), lambda i: (i, 0))
    if x_shift is None:
        per = tm // SUBLANES
        shift_in = x
        shift_spec = pl.BlockSpec((SUBLANES, d), lambda i: (jnp.maximum(i * per - 1, 0), 0))
    else:
        seq_len = None
        shift_in = x_shift
        shift_spec = tile
    acts = [x, shift_in] + ([v_first] if has_vres else [])
    kern = functools.partial(_tm_pre_kernel, has_vres=has_vres, seq_len=seq_len, tile=tm)
    out = jax.ShapeDtypeStruct((n, d), F32)
    return pl.pallas_call(
        kern,
        grid=(n // tm,),
        in_specs=[tile, shift_spec] + [tile] * has_vres + [_const_spec(w.shape) for w in weights],
        out_specs=[tile] * 7,
        out_shape=[out] * 7,
        compiler_params=_params(("parallel",)),
        name="tm_pre_vres" if has_vres else "tm_pre",
    )(*acts, *weights)


def _scan_kernel(r_ref, lw_ref, k_ref, v_ref, a_ref, b_ref, s0_ref, y_ref, sout_ref, s_scr,
                 *, chunk, n_chunks, n_groups):
    c = pl.program_id(1)
    hd = HEAD_DIM
    gw = MXU_W
    hpg = gw // hd
    pw = hpg * chunk

    @pl.when(c == 0)
    def _():
        s_scr[...] = jnp.zeros_like(s_scr)
        for g in range(n_groups):
            for h in range(hpg):
                s_scr[g, h * hd:(h + 1) * hd, h * hd:(h + 1) * hd] = s0_ref[0, g * hpg + h]

    row_p = _iota((chunk, pw), 0)
    j_p = _iota((chunk, pw), 1) % chunk
    strict = j_p < row_p
    incl = j_p <= row_p
    tri = jnp.where(_iota((chunk, chunk), 1) <= _iota((chunk, chunk), 0), 1.0, 0.0).astype(BF16)
    bd_pp = (_iota((pw, pw), 0) // chunk) == (_iota((pw, pw), 1) // chunk)
    bd_pg = (_iota((pw, gw), 0) // chunk) == (_iota((pw, gw), 1) // hd)
    bd_gg = (_iota((gw, gw), 0) // hd) == (_iota((gw, gw), 1) // hd)

    def heads_on_rows(x, mask):
        return jnp.where(mask, jnp.concatenate([x] * hpg, axis=0), 0.0).astype(BF16)

    groups = range(n_groups)
    sls = [slice(g * gw, (g + 1) * gw) for g in groups]
    vs, kbs, w_lasts, ars, r_bs, r_ks = [], [], [], [], [], []
    for sl in sls:
        r, lw, k, v, a, b = (ref[:, sl] for ref in (r_ref, lw_ref, k_ref, v_ref, a_ref, b_ref))
        hi = lw.astype(BF16)
        rem = lw - hi.astype(F32)
        mid = rem.astype(BF16)
        lo = (rem - mid.astype(F32)).astype(BF16)
        cum = (jnp.dot(tri, hi, preferred_element_type=F32) + jnp.dot(tri, mid, preferred_element_type=F32)
               + jnp.dot(tri, lo, preferred_element_type=F32))
        w_last = cum[chunk - 1:chunk, :]
        w_neg = jnp.exp(-cum)
        w_rest = jnp.exp(w_last - cum)
        vs.append(v)
        w_lasts.append(w_last)
        kbs.append(jnp.concatenate([k * w_rest, b * w_rest], axis=0))
        ars.append(jnp.concatenate([a * jnp.exp(cum - lw), r * jnp.exp(cum)], axis=0).astype(BF16))
        r_bs.append(heads_on_rows(b * w_neg, bd_pg))
        r_ks.append(heads_on_rows(k * w_neg, bd_pg))

    ss = [s_scr[g] for g in groups]
    g_bs = [_mm_nt(ars[g], r_bs[g]) for g in groups]
    g_ks = [_mm_nt(ars[g], r_ks[g]) for g in groups]
    from_state = [_mm_nt(ars[g], ss[g]) for g in groups]
    v_bds = [heads_on_rows(vs[g], bd_pg) for g in groups]
    a_abs = [jnp.where(strict, g_bs[g][:chunk], 0.0) for g in groups]
    a_rbs = [jnp.where(incl, g_bs[g][chunk:], 0.0) for g in groups]
    a_aks = [jnp.where(strict, g_ks[g][:chunk], 0.0) for g in groups]
    a_rks = [jnp.where(incl, g_ks[g][chunk:], 0.0) for g in groups]
    us = [from_state[g][:chunk] + _mm(a_aks[g], v_bds[g]) for g in groups]
    powers = a_abs
    span = 1
    while True:
        us = [us[g] + _mm(powers[g], heads_on_rows(us[g], bd_pg)) for g in groups]
        span *= 2
        if span >= chunk:
            break
        powers = [_mm(powers[g], heads_on_rows(powers[g], bd_pp)) for g in groups]
    for g in groups:
        y_ref[:, sls[g]] = (from_state[g][chunk:] + _mm(a_rks[g], v_bds[g])
                            + _mm(a_rbs[g], heads_on_rows(us[g], bd_pg)))
    for g in groups:
        vu_t = jnp.transpose(jnp.concatenate([vs[g], us[g]], axis=0))
        s_scr[g] = ss[g] * jnp.exp(w_lasts[g]) + jnp.where(bd_gg, _mm(vu_t, kbs[g]), 0.0)

    @pl.when(c == n_chunks - 1)
    def _():
        for g in range(n_groups):
            for h in range(hpg):
                sout_ref[0, g * hpg + h] = s_scr[g, h * hd:(h + 1) * hd, h * hd:(h + 1) * hd]


def _rwkv_scan(r, lw, k, v, a, b, s0, chunk):
    n, d = r.shape
    n_seq, n_heads = s0.shape[0], s0.shape[1]
    t = n // n_seq
    n_chunks = t // chunk
    n_groups = d // MXU_W
    tile = pl.BlockSpec((chunk, d), lambda bi, c: (bi * n_chunks + c, 0))
    st = pl.BlockSpec((1, n_heads, HEAD_DIM, HEAD_DIM), lambda bi, c: (bi, 0, 0, 0))
    kern = functools.partial(_scan_kernel, chunk=chunk, n_chunks=n_chunks, n_groups=n_groups)
    return pl.pallas_call(
        kern,
        grid=(n_seq, n_chunks),
        in_specs=[tile] * 6 + [st],
        out_specs=[tile, st],
        out_shape=[jax.ShapeDtypeStruct((n, d), F32), jax.ShapeDtypeStruct(s0.shape, F32)],
        scratch_shapes=[pltpu.VMEM((n_groups, MXU_W, MXU_W), F32)],
        compiler_params=_params(("parallel", "arbitrary")),
        name=f"rwkv_scan_c{chunk}",
    )(r, lw, k, v, a, b, s0)


def _tm_post_kernel(y_ref, r_ref, k_ref, v_ref, g_ref, x_ref, wo_ref, gng_ref, gnb_ref, rk_ref,
                    lng_ref, lnb_ref, sel_ref, selt_ref, o_ref, *, alpha):
    sel, selt = sel_ref[...], selt_ref[...]
    y = y_ref[...]
    inv = 1.0 / HEAD_DIM
    yc = y - _head_sum(y, sel, selt) * inv
    var = _head_sum(yc * yc, sel, selt) * inv
    yn = yc * lax.rsqrt(var + GN_EPS) * gng_ref[...] + gnb_ref[...]
    bonus = _head_sum(r_ref[...] * k_ref[...] * rk_ref[...], sel, selt) * v_ref[...]
    z = ((yn + bonus) * g_ref[...]).astype(BF16)
    out = jnp.dot(z, wo_ref[...], preferred_element_type=F32)
    o_ref[...] = _layer_norm(alpha * x_ref[...] + out, lng_ref[...], lnb_ref[...])


def _tm_post(y, r, k, v, g, x, weights, alpha):
    n, d = x.shape
    tm = PRE_TILE
    tile = pl.BlockSpec((tm, d), lambda i: (i, 0))
    return pl.pallas_call(
        functools.partial(_tm_post_kernel, alpha=alpha),
        grid=(n // tm,),
        in_specs=[tile] * 6 + [_const_spec(w.shape) for w in weights],
        out_specs=tile,
        out_shape=jax.ShapeDtypeStruct((n, d), F32),
        compiler_params=_params(("parallel",)),
        name="tm_post",
    )(y, r, k, v, g, x, *weights)


def _proj_kernel(x_ref, *refs):
    n_out = len(refs) // 2
    xb = x_ref[...].astype(BF16)
    for w_ref, o_ref in zip(refs[:n_out], refs[n_out:]):
        o_ref[...] = jnp.dot(xb, w_ref[...], preferred_element_type=F32)


def _proj(x, ws, name):
    n, d = x.shape
    tm = TOKEN_TILE
    tile = pl.BlockSpec((tm, d), lambda i: (i, 0))
    return pl.pallas_call(
        _proj_kernel,
        grid=(n // tm,),
        in_specs=[tile] + [_const_spec(w.shape) for w in ws],
        out_specs=[pl.BlockSpec((tm, w.shape[1]), lambda i: (i, 0)) for w in ws],
        out_shape=[jax.ShapeDtypeStruct((n, w.shape[1]), F32) for w in ws],
        compiler_params=_params(("parallel",)),
        name=name,
    )(x, *ws)


def _oproj_kernel(o_ref, x_ref, w_ref, g_ref, b_ref, out_ref, *, alpha):
    sub = jnp.dot(o_ref[...].astype(BF16), w_ref[...], preferred_element_type=F32)
    out_ref[...] = _layer_norm(alpha * x_ref[...] + sub, g_ref[...], b_ref[...])


def _oproj_postnorm(o, x, w, g, b, alpha):
    n, d = x.shape
    tm = TOKEN_TILE
    tile = pl.BlockSpec((tm, d), lambda i: (i, 0))
    return pl.pallas_call(
        functools.partial(_oproj_kernel, alpha=alpha),
        grid=(n // tm,),
        in_specs=[tile, tile, _const_spec(w.shape), _const_spec(g.shape), _const_spec(b.shape)],
        out_specs=tile,
        out_shape=jax.ShapeDtypeStruct((n, d), F32),
        compiler_params=_params(("parallel",)),
        name="oproj_postnorm",
    )(o, x, w, g, b)


LOG2E = 1.4426950408889634


def _log2_survive(nz2):
    sign = jnp.uint32(0x80000000)
    neg_abs = lax.bitcast_convert_type(lax.bitcast_convert_type(nz2, jnp.uint32) | sign, F32)
    return jnp.minimum(nz2, 0.0) - jnp.log2(1.0 + jnp.exp2(neg_abs))


def _attn_prompt_kernel(bias_ref, q_ref, k_ref, v_ref, o_ref, *, tq, tk, n_sub):
    p = pl.program_id(1)
    qb = pl.program_id(2)
    hd = HEAD_DIM
    q = (q_ref[...] * (-SB_SCALE * LOG2E)).astype(BF16)
    bias = jnp.where(_iota((1, 2 * tk), 1) < tk, bias_ref[2 * p], bias_ref[2 * p + 1]) * (-LOG2E)
    row = _iota((tq, tk), 0)
    col = _iota((tq, tk), 1)
    head0_k = _iota((tk, LANES), 1) < hd
    r2 = _iota((tk, 2 * tk), 0)
    c2 = _iota((tk, 2 * tk), 1)
    suffix = jnp.where((c2 >= tk) | (r2 > c2), 1.0, 0.0).astype(BF16)

    kpi = n_sub * tk
    per_q = tq // tk
    n_phase = kpi // tq
    diag = (qb * tq) // kpi

    def heads_on_rows(blk):
        return jnp.concatenate([jnp.where(head0_k, blk, 0.0), jnp.where(head0_k, 0.0, blk)], axis=0).astype(BF16)

    def key_blocks(start, carry, phase):
        acc, car0, car1 = carry
        n_act = n_sub if phase is None else (phase + 1) * per_q
        subs = list(reversed(range(n_act)))
        st = [dict() for _ in subs]

        def stage(i, s):
            sub, c = subs[i], st[i]
            masked = phase is not None and sub >= phase * per_q
            if s == 0:
                kbd = heads_on_rows(k_ref[pl.ds(start + sub * tk, tk), :])
                c["nz"] = lax.dot_general(q, kbd, NT, preferred_element_type=F32) + bias
            elif s == 1:
                l1m = _log2_survive(c["nz"])
                if masked:
                    c["valid"] = jnp.concatenate([(col + (sub - phase * per_q) * tk) < row] * 2, axis=1)
                    l1m = jnp.where(c["valid"], l1m, 0.0)
                c["l1m"] = l1m
            elif s == 2:
                l1m = c["l1m"].astype(BF16)
                c["cs"] = [jnp.dot(l1m[:, h * tk:(h + 1) * tk], suffix, preferred_element_type=F32)
                           for h in range(2)]
            elif s == 3:
                nonlocal car0, car1
                surv = jnp.concatenate([c["cs"][0][:, :tk] + car0, c["cs"][1][:, :tk] + car1], axis=1)
                w = jnp.exp2(c["l1m"] - c["nz"] + surv)
                if masked:
                    w = jnp.where(c["valid"], w, 0.0)
                c["w"] = w.astype(BF16)
                car0 = car0 + c["cs"][0][:, tk:]
                car1 = car1 + c["cs"][1][:, tk:]
            else:
                nonlocal acc
                vbd = heads_on_rows(v_ref[pl.ds(start + sub * tk, tk), :])
                acc = acc + jnp.dot(c["w"], vbd, preferred_element_type=F32)

        n_stages = 5
        for t in range(n_act + n_stages - 1):
            for i in range(n_act):
                if 0 <= t - i < n_stages:
                    stage(i, t - i)
        return acc, car0, car1

    zero = jnp.zeros((tq, LANES), F32)
    diag_start = pl.multiple_of(diag * kpi, kpi)
    carry = lax.switch(qb % n_phase,
                       [functools.partial(key_blocks, diag_start, (zero, zero, zero), ph)
                        for ph in range(n_phase)])

    def body(i, carry):
        return key_blocks(pl.multiple_of((diag - i) * kpi, kpi), carry, None)

    acc, _, _ = lax.fori_loop(1, diag + 1, body, carry)
    o_ref[...] = acc


def _attn_prompt(q, k, v, bias, n_seq):
    n, d = q.shape
    t = n // n_seq
    tq, tk, n_sub = ATT_Q, ATT_K, ATT_SUB
    assert tk == LANES and tq % tk == 0 and (n_sub * tk) % tq == 0 and t % (n_sub * tk) == 0
    n_blk = t // tq
    n_pairs = d // LANES
    return pl.pallas_call(
        functools.partial(_attn_prompt_kernel, tq=tq, tk=tk, n_sub=n_sub),
        grid_spec=pltpu.PrefetchScalarGridSpec(
            num_scalar_prefetch=1,
            grid=(n_seq, n_pairs, n_blk),
            in_specs=[
                pl.BlockSpec((tq, LANES), lambda bi, p, qb, bias_ref: (bi * n_blk + qb, p)),
                pl.BlockSpec((t, LANES), lambda bi, p, qb, bias_ref: (bi, p)),
                pl.BlockSpec((t, LANES), lambda bi, p, qb, bias_ref: (bi, p)),
            ],
            out_specs=pl.BlockSpec((tq, LANES), lambda bi, p, qb, bias_ref: (bi * n_blk + qb, p)),
        ),
        out_shape=jax.ShapeDtypeStruct((n, d), F32),
        compiler_params=_params(("parallel", "parallel", "arbitrary")),
        name="attn_prompt",
    )(bias, q, k, v)


def _kv_gather_kernel(pt_ref, kc_ref, vc_ref, ko_ref, vo_ref, *, n_heads):
    page = ko_ref.shape[1]
    for src, dst in ((kc_ref, ko_ref), (vc_ref, vo_ref)):
        for p in range(n_heads // 2):
            pair = jnp.concatenate([src[0, pl.ds(2 * p, page, stride=n_heads), :],
                                    src[0, pl.ds(2 * p + 1, page, stride=n_heads), :]], axis=1)
            dst[0, :, p * LANES:(p + 1) * LANES] = pair.astype(BF16)


def _kv_gather(cache_k, cache_v, page_table):
    n_phys, page, n_heads, hd = cache_k.shape
    n_seq, n_pages = page_table.shape
    d = n_heads * hd
    src = pl.BlockSpec((1, page * n_heads, hd), lambda bi, j, pt_ref: (pt_ref[bi, j], 0, 0))
    dst = pl.BlockSpec((1, page, d), lambda bi, j, pt_ref: (bi, j, 0))
    out = jax.ShapeDtypeStruct((n_seq, n_pages * page, d), BF16)
    return pl.pallas_call(
        functools.partial(_kv_gather_kernel, n_heads=n_heads),
        grid_spec=pltpu.PrefetchScalarGridSpec(
            num_scalar_prefetch=1,
            grid=(n_seq, n_pages),
            in_specs=[src, src],
            out_specs=[dst, dst],
        ),
        out_shape=[out, out],
        compiler_params=_params(("parallel", "arbitrary")),
        name="kv_gather",
    )(page_table, cache_k.reshape(n_phys, page * n_heads, hd), cache_v.reshape(n_phys, page * n_heads, hd))


def _attn_sample_kernel(q_ref, kn_ref, vn_ref, kc_ref, vc_ref, biasc_ref, o_ref,
                        qrep_scr, kpad_scr, vpad_scr, acc_scr, carry_scr, *, n_new, n_steps, n_pg):
    j = pl.program_id(1)
    page = LANES
    d = q_ref.shape[2]
    n_rows = (d // HEAD_DIM) * n_new
    own_head = (_iota((n_rows, d), 0) // n_new) == (_iota((n_rows, d), 1) // HEAD_DIM)
    r2 = _iota((page, 2 * page), 0)
    c2 = _iota((page, 2 * page), 1)
    suffix = jnp.where((c2 >= page) | (r2 > c2), 1.0, 0.0).astype(BF16)

    def visit(k_cat, v_cat, n_blk, valid):
        bias = jnp.concatenate([biasc_ref[...]] * n_blk, axis=1)
        nz = lax.dot_general(qrep_scr[...], k_cat, NT, preferred_element_type=F32) + bias
        l1m = _log2_survive(nz)
        if valid is not None:
            l1m = jnp.where(valid, l1m, 0.0)
        blks = [slice(i * page, (i + 1) * page) for i in range(n_blk)]
        cs = [jnp.dot(l1m[:, sl].astype(BF16), suffix, preferred_element_type=F32) for sl in blks]
        carry = carry_scr[...]
        ws = [None] * n_blk
        for i in reversed(range(n_blk)):
            ws[i] = jnp.exp2(l1m[:, blks[i]] - nz[:, blks[i]] + cs[i][:, :page] + carry)
            carry = carry + cs[i][:, page:]
        carry_scr[...] = carry
        w = jnp.concatenate(ws, axis=1)
        if valid is not None:
            w = jnp.where(valid, w, 0.0)
        acc_scr[...] += jnp.dot(w.astype(BF16), v_cat, preferred_element_type=F32)

    @pl.when(j == 0)
    def _():
        rr = _iota((n_rows, SAMPLE_PAD), 0)
        cc = _iota((n_rows, SAMPLE_PAD), 1)
        rep = jnp.where((rr % n_new) == cc, 1.0, 0.0).astype(BF16)
        qrep = jnp.dot(rep, (q_ref[0] * (-SB_SCALE * LOG2E)).astype(BF16), preferred_element_type=F32)
        qrep_scr[...] = jnp.where(own_head, qrep, 0.0).astype(BF16)
        kpad_scr[...] = jnp.zeros_like(kpad_scr)
        vpad_scr[...] = jnp.zeros_like(vpad_scr)
        kpad_scr[0:SAMPLE_PAD, :] = kn_ref[0].astype(BF16)
        vpad_scr[0:SAMPLE_PAD, :] = vn_ref[0].astype(BF16)
        acc_scr[...] = jnp.zeros_like(acc_scr)
        carry_scr[...] = jnp.zeros_like(carry_scr)
        valid = _iota((n_rows, page), 1) < (_iota((n_rows, page), 0) % n_new)
        visit(kpad_scr[...], vpad_scr[...], 1, valid)

    @pl.when(j > 0)
    def _():
        visit(kc_ref[0], vc_ref[0], n_pg, None)

    @pl.when(j == n_steps - 1)
    def _():
        rr = _iota((SAMPLE_PAD, n_rows), 0)
        cc = _iota((SAMPLE_PAD, n_rows), 1)
        pick = jnp.where((cc % n_new) == rr, 1.0, 0.0)
        own = jnp.where(own_head, acc_scr[...], 0.0)
        o_ref[0] = jnp.dot(pick, own, preferred_element_type=F32, precision=HI)


def _attn_sample(q, k_new, v_new, k_past, v_past, bias_rows, n_new):
    n_seq, pad, d = q.shape
    page = LANES
    n_pages = k_past.shape[1] // page
    n_pg = min(PAGES_PER_STEP, n_pages)
    assert n_pages % n_pg == 0
    n_blocks = n_pages // n_pg
    n_steps = n_blocks + 1
    row = pl.BlockSpec((1, pad, d), lambda bi, j: (bi, 0, 0))
    past = pl.BlockSpec((1, n_pg * page, d), lambda bi, j: (bi, n_blocks - jnp.maximum(j, 1), 0))
    n_rows = bias_rows.shape[0]
    return pl.pallas_call(
        functools.partial(_attn_sample_kernel, n_new=n_new, n_steps=n_steps, n_pg=n_pg),
        grid=(n_seq, n_steps),
        in_specs=[row, row, row, past, past, pl.BlockSpec((n_rows, LANES), lambda bi, j: (0, 0))],
        out_specs=row,
        scratch_shapes=[
            pltpu.VMEM((n_rows, d), BF16),
            pltpu.VMEM((page, d), BF16),
            pltpu.VMEM((page, d), BF16),
            pltpu.VMEM((n_rows, d), F32),
            pltpu.VMEM((n_rows, LANES), F32),
        ],
        out_shape=jax.ShapeDtypeStruct((n_seq, pad, d), F32),
        compiler_params=_params(("parallel", "arbitrary")),
        name="attn_sample",
    )(q, k_new, v_new, k_past, v_past, bias_rows)


def kernel(x_prompt, x_sample, cache_k, cache_v, state_wkv, state_shift, page_table, ln_g, ln_b, ffn_w_gate, ffn_w_up, ffn_w_down, tm_mu, tm_w0, tm_w1, tm_w2, tm_a0, tm_a1, tm_a2, tm_v0, tm_v1, tm_v2, tm_g1, tm_g2, tm_k_k, tm_k_a, tm_r_k, tm_w_r, tm_w_k, tm_w_v, tm_w_o, tm_gn_g, tm_gn_b, sb_w_k, sb_w_v, sb_w_q, sb_bias, sb_w_o):
    bp, sp, d = x_prompt.shape
    bs, ss, _ = x_sample.shape
    depth = ln_g.shape[0]
    n_a = tm_mu.shape[0]
    n_heads = d // HEAD_DIM
    n_ff = ffn_w_gate.shape[-1] // FFN_CHUNK
    alpha = (2 * depth) ** 0.25

    def vec(a):
        return a.reshape(1, -1).astype(F32)

    def bf(a):
        return a.astype(BF16)

    hid = jnp.arange(d) // HEAD_DIM
    sel = (hid[:, None] == jnp.arange(LANES)[None, :]).astype(BF16)
    sel_t = sel.T

    ffn_w = {}
    for li in range(depth):
        for j in range(2):
            ffn_w[li, j] = (
                bf(ffn_w_gate[li, j]).reshape(d, n_ff, FFN_CHUNK).transpose(1, 0, 2),
                bf(ffn_w_up[li, j]).reshape(d, n_ff, FFN_CHUNK).transpose(1, 0, 2),
                bf(ffn_w_down[li, j]).reshape(n_ff, FFN_CHUNK, d),
                vec(ln_g[li, 2 * j]), vec(ln_b[li, 2 * j]))
    pre_w, post_w = [], []
    for li in range(n_a):
        w = [tm_mu[li], bf(tm_w_r[li]), bf(tm_w_k[li]), bf(tm_w_v[li]),
             vec(tm_w0[li]), bf(tm_w1[li]), bf(tm_w2[li]), vec(tm_a0[li]), bf(tm_a1[li]), bf(tm_a2[li])]
        if li > 0:
            w += [vec(tm_v0[li - 1]), bf(tm_v1[li - 1]), bf(tm_v2[li - 1])]
        w += [bf(tm_g1[li]), bf(tm_g2[li]), vec(tm_k_k[li]), vec(tm_k_a[li]), sel, sel_t]
        pre_w.append(w)
        post_w.append([bf(tm_w_o[li]), vec(tm_gn_g[li]), vec(tm_gn_b[li]), vec(tm_r_k[li]),
                       vec(ln_g[li, 1]), vec(ln_b[li, 1]), sel, sel_t])
    wq = [bf(sb_w_q[j]) for j in range(depth - n_a)]
    wo = [bf(sb_w_o[j]) for j in range(depth - n_a)]
    wkv = [bf(sb_w_k), bf(sb_w_v)]
    k_past, v_past = _kv_gather(cache_k, cache_v, page_table)

    def trunk(x, n_seq, t, wkv_in, shift_in):
        fresh = shift_in is None
        n = n_seq * t
        new_wkv, new_shift = [], []
        v_first = None
        k_sh = v_sh = None

        def pad_rows(a):
            a = a.reshape(n_seq, t, d)
            return jnp.pad(a, ((0, 0), (0, SAMPLE_PAD - t), (0, 0)))

        for li in range(depth):
            x = _ffn_half(x, *ffn_w[li, 0], alpha)
            if li < n_a:
                x3 = x.reshape(n_seq, t, d)
                new_shift.append(x3[:, -1])
                if fresh:
                    x_shift = None
                else:
                    x_shift = jnp.concatenate([shift_in[li][:, None, :], x3[:, :-1]], axis=1).reshape(n, d)
                r, lw, k, v, a, b, g = _tm_pre(x, x_shift, v_first if li > 0 else None, pre_w[li], t)
                if li == 0:
                    v_first = v
                if fresh:
                    y, s_new = _rwkv_scan(r, lw, k, v, a, b, wkv_in[li], SCAN_CHUNK)
                else:
                    padded = (pad_rows(z).reshape(n_seq * SAMPLE_PAD, d) for z in (r, lw, k, v, a, b))
                    y, s_new = _rwkv_scan(*padded, wkv_in[li], SAMPLE_PAD)
                    y = y.reshape(n_seq, SAMPLE_PAD, d)[:, :t].reshape(n, d)
                new_wkv.append(s_new)
                x = _tm_post(y, r, k, v, g, x, post_w[li], alpha)
            else:
                j = li - n_a
                (q,) = _proj(x, [wq[j]], "q_proj")
                if fresh:
                    o = _attn_prompt(q, k_sh, v_sh, sb_bias[j].astype(F32), n_seq)
                else:
                    nbias2 = sb_bias[j].astype(F32) * (-LOG2E)
                    bias_rows = jnp.broadcast_to(jnp.repeat(nbias2, t)[:, None], (n_heads * t, LANES))
                    o = _attn_sample(pad_rows(q), pad_rows(k_sh), pad_rows(v_sh), k_past, v_past,
                                     bias_rows, t)
                    o = o[:, :t].reshape(n, d)
                x = _oproj_postnorm(o, x, wo[j], vec(ln_g[li, 1]), vec(ln_b[li, 1]), alpha)
            x = _ffn_half(x, *ffn_w[li, 1], alpha)
            if li == n_a - 1:
                k_sh, v_sh = _proj(x, wkv, "kv_proj")
        return x, jnp.stack(new_wkv), jnp.stack(new_shift), k_sh, v_sh

    zero_state = jnp.zeros((n_a, bp, n_heads, HEAD_DIM, HEAD_DIM), F32)
    y_p, wkv_p, shift_p, k_p, v_p = trunk(x_prompt.reshape(bp * sp, d), bp, sp, zero_state, None)
    y_s, wkv_s, shift_s, k_s, v_s = trunk(x_sample.reshape(bs * ss, d), bs, ss, state_wkv, state_shift)

    def paged(a):
        return a.reshape(bp, sp // PAGE_SIZE, PAGE_SIZE, n_heads, HEAD_DIM)

    return (y_p.reshape(bp, sp, d), y_s.reshape(bs, ss, d), wkv_p, shift_p, paged(k_p), paged(v_p),
            wkv_s, shift_s, k_s.reshape(bs, ss, n_heads, HEAD_DIM), v_s.reshape(bs, ss, n_heads, HEAD_DIM))
```

```python
import functools

import jax
import jax.numpy as jnp
from jax import lax
from jax.experimental import pallas as pl
from jax.experimental.pallas import tpu as pltpu

HEAD_DIM = 64
LANES = 128
SUBLANES = 8
MXU_W = 256
LN_EPS = 1e-5
GN_EPS = 64e-5
SB_SCALE = HEAD_DIM ** -0.5
PAGE_SIZE = 128

FFN_CHUNK = MXU_W
TOKEN_TILE = 512
PRE_TILE = 256
SCAN_CHUNK = 64
SAMPLE_PAD = SUBLANES
ATT_Q = 256
ATT_K = LANES
ATT_SUB = 4
PAGES_PER_STEP = 8
VMEM_LIMIT = 56 * 1024 * 1024

F32 = jnp.float32
BF16 = jnp.bfloat16
HI = lax.Precision.HIGHEST
NT = (((1,), (1,)), ((), ()))


def _const_spec(shape):
    nd = len(shape)
    return pl.BlockSpec(shape, lambda *_: (0,) * nd, pipeline_mode=pl.Buffered(1))


def _params(sem):
    return pltpu.CompilerParams(dimension_semantics=sem, vmem_limit_bytes=VMEM_LIMIT)


def _iota(shape, axis):
    return lax.broadcasted_iota(jnp.int32, shape, axis)


def _layer_norm(y, g, b):
    mu = jnp.mean(y, axis=-1, keepdims=True)
    yc = y - mu
    var = jnp.mean(yc * yc, axis=-1, keepdims=True)
    return yc * lax.rsqrt(var + LN_EPS) * g + b


def _sigmoid(x):
    return 1.0 / (1.0 + jnp.exp(-x))


def _split_bf16(x):
    hi = x.astype(BF16)
    lo = (x - hi.astype(F32)).astype(BF16)
    return hi, lo


def _mm(a, b):
    return jnp.dot(a.astype(BF16), b.astype(BF16), preferred_element_type=F32)


def _mm_nt(a, b):
    return lax.dot_general(a.astype(BF16), b.astype(BF16), NT, preferred_element_type=F32)


def _head_sum(q, sel, sel_t):
    hi, lo = _split_bf16(q)
    s = jnp.dot(hi, sel, preferred_element_type=F32) + jnp.dot(lo, sel, preferred_element_type=F32)
    hi, lo = _split_bf16(s)
    return jnp.dot(hi, sel_t, preferred_element_type=F32) + jnp.dot(lo, sel_t, preferred_element_type=F32)


def _ffn_kernel(x_ref, wg_ref, wu_ref, wd_ref, g_ref, b_ref, o_ref, acc_ref, *, alpha, n_chunks):
    x = x_ref[...]
    xb = x.astype(BF16)
    acc_ref[...] = jnp.zeros_like(acc_ref)

    def body(c, carry):
        gate = jnp.dot(xb, wg_ref[c], preferred_element_type=F32)
        up = jnp.dot(xb, wu_ref[c], preferred_element_type=F32)
        h = (gate * _sigmoid(gate) * up).astype(BF16)
        acc_ref[...] += jnp.dot(h, wd_ref[c], preferred_element_type=F32)
        return carry

    lax.fori_loop(0, n_chunks, body, 0)
    y = alpha * x + 0.5 * acc_ref[...]
    o_ref[...] = _layer_norm(y, g_ref[...], b_ref[...])


def _ffn_half(x, wg, wu, wd, g, b, alpha):
    n, d = x.shape
    n_chunks = wg.shape[0]
    tm = TOKEN_TILE
    kern = functools.partial(_ffn_kernel, alpha=alpha, n_chunks=n_chunks)
    return pl.pallas_call(
        kern,
        grid=(n // tm,),
        in_specs=[
            pl.BlockSpec((tm, d), lambda i: (i, 0)),
            _const_spec(wg.shape), _const_spec(wu.shape), _const_spec(wd.shape),
            _const_spec(g.shape), _const_spec(b.shape),
        ],
        out_specs=pl.BlockSpec((tm, d), lambda i: (i, 0)),
        out_shape=jax.ShapeDtypeStruct((n, d), F32),
        scratch_shapes=[pltpu.VMEM((tm, d), F32)],
        compiler_params=_params(("parallel",)),
        name="ffn_half",
    )(x, wg, wu, wd, g, b)


def _tm_pre_kernel(*refs, has_vres, seq_len, tile):
    x_ref, shift_ref = refs[0], refs[1]
    refs = refs[2:]
    if has_vres:
        vf_ref, refs = refs[0], refs[1:]
    (mu_ref, wr_ref, wk_ref, wv_ref, w0_ref, w1_ref, w2_ref, a0_ref, a1_ref, a2_ref) = refs[:10]
    refs = refs[10:]
    if has_vres:
        (v0_ref, v1_ref, v2_ref), refs = refs[:3], refs[3:]
    (g1_ref, g2_ref, kk_ref, ka_ref, sel_ref, selt_ref,
     r_out, lw_out, k_out, v_out, a_out, b_out, g_out) = refs
    x = x_ref[...]
    if seq_len is None:
        xs = shift_ref[...]
    else:
        starts_sequence = (pl.program_id(0) * tile) % seq_len == 0
        first = jnp.where(starts_sequence, 0.0, shift_ref[SUBLANES - 1:SUBLANES, :])
        xs = jnp.where(_iota(x.shape, 0) == 0, first, pltpu.roll(x, 1, 0))
    xx = xs - x

    def mix(i):
        return (x + xx * mu_ref[i:i + 1, :]).astype(BF16)

    def mm(a, w_ref):
        return jnp.dot(a, w_ref[...], preferred_element_type=F32)

    xr, xw, xk, xv, xa, xg = (mix(i) for i in range(6))
    r_out[...] = mm(xr, wr_ref)
    w_in = w0_ref[...] + mm(jnp.tanh(mm(xw, w1_ref)).astype(BF16), w2_ref)
    softplus = jnp.maximum(-w_in, 0.0) + jnp.log1p(jnp.exp(-jnp.abs(w_in)))
    lw_out[...] = -jnp.exp(-softplus - 0.5)
    k = mm(xk, wk_ref)
    v = mm(xv, wv_ref)
    if has_vres:
        gate = _sigmoid(v0_ref[...] + mm(mm(xv, v1_ref).astype(BF16), v2_ref))
        v = v + (vf_ref[...] - v) * gate
    v_out[...] = v
    a = _sigmoid(a0_ref[...] + mm(mm(xa, a1_ref).astype(BF16), a2_ref))
    g_out[...] = mm(_sigmoid(mm(xg, g1_ref)).astype(BF16), g2_ref)
    kk = k * kk_ref[...]
    norm = jnp.sqrt(_head_sum(kk * kk, sel_ref[...], selt_ref[...]))
    kk = kk / jnp.maximum(norm, 1e-12)
    k_out[...] = k * (1.0 + (a - 1.0) * ka_ref[...])
    a_out[...] = -kk
    b_out[...] = kk * a


def _tm_pre(x, x_shift, v_first, weights, seq_len):
    n, d = x.shape
    tm = PRE_TILE
    has_vres = v_first is not None
    tile = pl.BlockSpec((tm, d), lambda i: (i, 0))
    if x_shift is None:
        per = tm // SUBLANES
        shift_in = x
        shift_spec = pl.BlockSpec((SUBLANES, d), lambda i: (jnp.maximum(i * per - 1, 0), 0))
    else:
        seq_len = None
        shift_in = x_shift
        shift_spec = tile
    acts = [x, shift_in] + ([v_first] if has_vres else [])
    kern = functools.partial(_tm_pre_kernel, has_vres=has_vres, seq_len=seq_len, tile=tm)
    out = jax.ShapeDtypeStruct((n, d), F32)
    return pl.pallas_call(
        kern,
        grid=(n // tm,),
        in_specs=[tile, shift_spec] + [tile] * has_vres + [_const_spec(w.shape) for w in weights],
        out_specs=[tile] * 7,
        out_shape=[out] * 7,
        compiler_params=_params(("parallel",)),
        name="tm_pre_vres" if has_vres else "tm_pre",
    )(*acts, *weights)


def _scan_kernel(r_ref, lw_ref, k_ref, v_ref, a_ref, b_ref, s0_ref, y_ref, sout_ref, s_scr,
                 *, chunk, n_chunks, n_groups):
    c = pl.program_id(1)
    hd = HEAD_DIM
    gw = MXU_W
    hpg = gw // hd
    pw = hpg * chunk

    @pl.when(c == 0)
    def _():
        s_scr[...] = jnp.zeros_like(s_scr)
        for g in range(n_groups):
            for h in range(hpg):
                s_scr[g, h * hd:(h + 1) * hd, h * hd:(h + 1) * hd] = s0_ref[0, g * hpg + h]

    row_p = _iota((chunk, pw), 0)
    j_p = _iota((chunk, pw), 1) % chunk
    strict = j_p < row_p
    incl = j_p <= row_p
    tri = jnp.where(_iota((chunk, chunk), 1) <= _iota((chunk, chunk), 0), 1.0, 0.0).astype(BF16)
    bd_pp = (_iota((pw, pw), 0) // chunk) == (_iota((pw, pw), 1) // chunk)
    bd_pg = (_iota((pw, gw), 0) // chunk) == (_iota((pw, gw), 1) // hd)
    bd_gg = (_iota((gw, gw), 0) // hd) == (_iota((gw, gw), 1) // hd)

    def heads_on_rows(x, mask):
        return jnp.where(mask, jnp.concatenate([x] * hpg, axis=0), 0.0).astype(BF16)

    groups = range(n_groups)
    sls = [slice(g * gw, (g + 1) * gw) for g in groups]
    vs, kbs, w_lasts, ars, r_bs, r_ks = [], [], [], [], [], []
    for sl in sls:
        r, lw, k, v, a, b = (ref[:, sl] for ref in (r_ref, lw_ref, k_ref, v_ref, a_ref, b_ref))
        hi = lw.astype(BF16)
        rem = lw - hi.astype(F32)
        mid = rem.astype(BF16)
        lo = (rem - mid.astype(F32)).astype(BF16)
        cum = (jnp.dot(tri, hi, preferred_element_type=F32) + jnp.dot(tri, mid, preferred_element_type=F32)
               + jnp.dot(tri, lo, preferred_element_type=F32))
        w_last = cum[chunk - 1:chunk, :]
        w_neg = jnp.exp(-cum)
        w_rest = jnp.exp(w_last - cum)
        vs.append(v)
        w_lasts.append(w_last)
        kbs.append(jnp.concatenate([k * w_rest, b * w_rest], axis=0))
        ars.append(jnp.concatenate([a * jnp.exp(cum - lw), r * jnp.exp(cum)], axis=0).astype(BF16))
        r_bs.append(heads_on_rows(b * w_neg, bd_pg))
        r_ks.append(heads_on_rows(k * w_neg, bd_pg))

    ss = [s_scr[g] for g in groups]
    g_bs = [_mm_nt(ars[g], r_bs[g]) for g in groups]
    g_ks = [_mm_nt(ars[g], r_ks[g]) for g in groups]
    from_state = [_mm_nt(ars[g], ss[g]) for g in groups]
    v_bds = [heads_on_rows(vs[g], bd_pg) for g in groups]
    a_abs = [jnp.where(strict, g_bs[g][:chunk], 0.0) for g in groups]
    a_rbs = [jnp.where(incl, g_bs[g][chunk:], 0.0) for g in groups]
    a_aks = [jnp.where(strict, g_ks[g][:chunk], 0.0) for g in groups]
    a_rks = [jnp.where(incl, g_ks[g][chunk:], 0.0) for g in groups]
    us = [from_state[g][:chunk] + _mm(a_aks[g], v_bds[g]) for g in groups]
    powers = a_abs
    span = 1
    while True:
        us = [us[g] + _mm(powers[g], heads_on_rows(us[g], bd_pg)) for g in groups]
        span *= 2
        if span >= chunk:
            break
        powers = [_mm(powers[g], heads_on_rows(powers[g], bd_pp)) for g in groups]
    for g in groups:
        y_ref[:, sls[g]] = (from_state[g][chunk:] + _mm(a_rks[g], v_bds[g])
                            + _mm(a_rbs[g], heads_on_rows(us[g], bd_pg)))
    for g in groups:
        vu_t = jnp.transpose(jnp.concatenate([vs[g], us[g]], axis=0))
        s_scr[g] = ss[g] * jnp.exp(w_lasts[g]) + jnp.where(bd_gg, _mm(vu_t, kbs[g]), 0.0)

    @pl.when(c == n_chunks - 1)
    def _():
        for g in range(n_groups):
            for h in range(hpg):
                sout_ref[0, g * hpg + h] = s_scr[g, h * hd:(h + 1) * hd, h * hd:(h + 1) * hd]


def _rwkv_scan(r, lw, k, v, a, b, s0, chunk):
    n, d = r.shape
    n_seq, n_heads = s0.shape[0], s0.shape[1]
    t = n // n_seq
    n_chunks = t // chunk
    n_groups = d // MXU_W
    tile = pl.BlockSpec((chunk, d), lambda bi, c: (bi * n_chunks + c, 0))
    st = pl.BlockSpec((1, n_heads, HEAD_DIM, HEAD_DIM), lambda bi, c: (bi, 0, 0, 0))
    kern = functools.partial(_scan_kernel, chunk=chunk, n_chunks=n_chunks, n_groups=n_groups)
    return pl.pallas_call(
        kern,
        grid=(n_seq, n_chunks),
        in_specs=[tile] * 6 + [st],
        out_specs=[tile, st],
        out_shape=[jax.ShapeDtypeStruct((n, d), F32), jax.ShapeDtypeStruct(s0.shape, F32)],
        scratch_shapes=[pltpu.VMEM((n_groups, MXU_W, MXU_W), F32)],
        compiler_params=_params(("parallel", "arbitrary")),
        name=f"rwkv_scan_c{chunk}",
    )(r, lw, k, v, a, b, s0)


def _tm_post_kernel(y_ref, r_ref, k_ref, v_ref, g_ref, x_ref, wo_ref, gng_ref, gnb_ref, rk_ref,
                    lng_ref, lnb_ref, sel_ref, selt_ref, o_ref, *, alpha):
    sel, selt = sel_ref[...], selt_ref[...]
    y = y_ref[...]
    inv = 1.0 / HEAD_DIM
    yc = y - _head_sum(y, sel, selt) * inv
    var = _head_sum(yc * yc, sel, selt) * inv
    yn = yc * lax.rsqrt(var + GN_EPS) * gng_ref[...] + gnb_ref[...]
    bonus = _head_sum(r_ref[...] * k_ref[...] * rk_ref[...], sel, selt) * v_ref[...]
    z = ((yn + bonus) * g_ref[...]).astype(BF16)
    out = jnp.dot(z, wo_ref[...], preferred_element_type=F32)
    o_ref[...] = _layer_norm(alpha * x_ref[...] + out, lng_ref[...], lnb_ref[...])


def _tm_post(y, r, k, v, g, x, weights, alpha):
    n, d = x.shape
    tm = PRE_TILE
    tile = pl.BlockSpec((tm, d), lambda i: (i, 0))
    return pl.pallas_call(
        functools.partial(_tm_post_kernel, alpha=alpha),
        grid=(n // tm,),
        in_specs=[tile] * 6 + [_const_spec(w.shape) for w in weights],
        out_specs=tile,
        out_shape=jax.ShapeDtypeStruct((n, d), F32),
        compiler_params=_params(("parallel",)),
        name="tm_post",
    )(y, r, k, v, g, x, *weights)


def _proj_kernel(x_ref, *refs):
    n_out = len(refs) // 2
    xb = x_ref[...].astype(BF16)
    for w_ref, o_ref in zip(refs[:n_out], refs[n_out:]):
        o_ref[...] = jnp.dot(xb, w_ref[...], preferred_element_type=F32)


def _proj(x, ws, name):
    n, d = x.shape
    tm = TOKEN_TILE
    tile = pl.BlockSpec((tm, d), lambda i: (i, 0))
    return pl.pallas_call(
        _proj_kernel,
        grid=(n // tm,),
        in_specs=[tile] + [_const_spec(w.shape) for w in ws],
        out_specs=[pl.BlockSpec((tm, w.shape[1]), lambda i: (i, 0)) for w in ws],
        out_shape=[jax.ShapeDtypeStruct((n, w.shape[1]), F32) for w in ws],
        compiler_params=_params(("parallel",)),
        name=name,
    )(x, *ws)


def _oproj_kernel(o_ref, x_ref, w_ref, g_ref, b_ref, out_ref, *, alpha):
    sub = jnp.dot(o_ref[...].astype(BF16), w_ref[...], preferred_element_type=F32)
    out_ref[...] = _layer_norm(alpha * x_ref[...] + sub, g_ref[...], b_ref[...])


def _oproj_postnorm(o, x, w, g, b, alpha):
    n, d = x.shape
    tm = TOKEN_TILE
    tile = pl.BlockSpec((tm, d), lambda i: (i, 0))
    return pl.pallas_call(
        functools.partial(_oproj_kernel, alpha=alpha),
        grid=(n // tm,),
        in_specs=[tile, tile, _const_spec(w.shape), _const_spec(g.shape), _const_spec(b.shape)],
        out_specs=tile,
        out_shape=jax.ShapeDtypeStruct((n, d), F32),
        compiler_params=_params(("parallel",)),
        name="oproj_postnorm",
    )(o, x, w, g, b)


LOG2E = 1.4426950408889634


def _log2_survive(nz2):
    sign = jnp.uint32(0x80000000)
    neg_abs = lax.bitcast_convert_type(lax.bitcast_convert_type(nz2, jnp.uint32) | sign, F32)
    return jnp.minimum(nz2, 0.0) - jnp.log2(1.0 + jnp.exp2(neg_abs))


def _attn_prompt_kernel(bias_ref, q_ref, k_ref, v_ref, o_ref, *, tq, tk, n_sub):
    p = pl.program_id(1)
    qb = pl.program_id(2)
    hd = HEAD_DIM
    q = (q_ref[...] * (-SB_SCALE * LOG2E)).astype(BF16)
    bias = jnp.where(_iota((1, 2 * tk), 1) < tk, bias_ref[2 * p], bias_ref[2 * p + 1]) * (-LOG2E)
    row = _iota((tq, tk), 0)
    col = _iota((tq, tk), 1)
    head0_k = _iota((tk, LANES), 1) < hd
    r2 = _iota((tk, 2 * tk), 0)
    c2 = _iota((tk, 2 * tk), 1)
    suffix = jnp.where((c2 >= tk) | (r2 > c2), 1.0, 0.0).astype(BF16)

    kpi = n_sub * tk
    per_q = tq // tk
    n_phase = kpi // tq
    diag = (qb * tq) // kpi

    def heads_on_rows(blk):
        return jnp.concatenate([jnp.where(head0_k, blk, 0.0), jnp.where(head0_k, 0.0, blk)], axis=0).astype(BF16)

    def key_blocks(start, carry, phase):
        acc, car0, car1 = carry
        n_act = n_sub if phase is None else (phase + 1) * per_q
        subs = list(reversed(range(n_act)))
        st = [dict() for _ in subs]

        def stage(i, s):
            sub, c = subs[i], st[i]
            masked = phase is not None and sub >= phase * per_q
            if s == 0:
                kbd = heads_on_rows(k_ref[pl.ds(start + sub * tk, tk), :])
                c["nz"] = lax.dot_general(q, kbd, NT, preferred_element_type=F32) + bias
            elif s == 1:
                l1m = _log2_survive(c["nz"])
                if masked:
                    c["valid"] = jnp.concatenate([(col + (sub - phase * per_q) * tk) < row] * 2, axis=1)
                    l1m = jnp.where(c["valid"], l1m, 0.0)
                c["l1m"] = l1m
            elif s == 2:
                l1m = c["l1m"].astype(BF16)
                c["cs"] = [jnp.dot(l1m[:, h * tk:(h + 1) * tk], suffix, preferred_element_type=F32)
                           for h in range(2)]
            elif s == 3:
                nonlocal car0, car1
                surv = jnp.concatenate([c["cs"][0][:, :tk] + car0, c["cs"][1][:, :tk] + car1], axis=1)
                w = jnp.exp2(c["l1m"] - c["nz"] + surv)
                if masked:
                    w = jnp.where(c["valid"], w, 0.0)
                c["w"] = w.astype(BF16)
                car0 = car0 + c["cs"][0][:, tk:]
                car1 = car1 + c["cs"][1][:, tk:]
            else:
                nonlocal acc
                vbd = heads_on_rows(v_ref[pl.ds(start + sub * tk, tk), :])
                acc = acc + jnp.dot(c["w"], vbd, preferred_element_type=F32)

        n_stages = 5
        for t in range(n_act + n_stages - 1):
            for i in range(n_act):
                if 0 <= t - i < n_stages:
                    stage(i, t - i)
        return acc, car0, car1

    zero = jnp.zeros((tq, LANES), F32)
    diag_start = pl.multiple_of(diag * kpi, kpi)
    carry = lax.switch(qb % n_phase,
                       [functools.partial(key_blocks, diag_start, (zero, zero, zero), ph)
                        for ph in range(n_phase)])

    def body(i, carry):
        return key_blocks(pl.multiple_of((diag - i) * kpi, kpi), carry, None)

    acc, _, _ = lax.fori_loop(1, diag + 1, body, carry)
    o_ref[...] = acc


def _attn_prompt(q, k, v, bias, n_seq):
    n, d = q.shape
    t = n // n_seq
    tq, tk, n_sub = ATT_Q, ATT_K, ATT_SUB
    assert tk == LANES and tq % tk == 0 and (n_sub * tk) % tq == 0 and t % (n_sub * tk) == 0
    n_blk = t // tq
    n_pairs = d // LANES
    return pl.pallas_call(
        functools.partial(_attn_prompt_kernel, tq=tq, tk=tk, n_sub=n_sub),
        grid_spec=pltpu.PrefetchScalarGridSpec(
            num_scalar_prefetch=1,
            grid=(n_seq, n_pairs, n_blk),
            in_specs=[
                pl.BlockSpec((tq, LANES), lambda bi, p, qb, bias_ref: (bi * n_blk + qb, p)),
                pl.BlockSpec((t, LANES), lambda bi, p, qb, bias_ref: (bi, p)),
                pl.BlockSpec((t, LANES), lambda bi, p, qb, bias_ref: (bi, p)),
            ],
            out_specs=pl.BlockSpec((tq, LANES), lambda bi, p, qb, bias_ref: (bi * n_blk + qb, p)),
        ),
        out_shape=jax.ShapeDtypeStruct((n, d), F32),
        compiler_params=_params(("parallel", "parallel", "arbitrary")),
        name="attn_prompt",
    )(bias, q, k, v)


def _attn_sample_kernel(pt_ref, q_ref, kn_ref, vn_ref, *refs, n_new, n_steps, n_pg):
    kc_refs, vc_refs = refs[:n_pg], refs[n_pg:2 * n_pg]
    biasc_ref, o_ref, qrep_scr, kpad_scr, vpad_scr, acc_scr, carry_scr = refs[2 * n_pg:]
    j = pl.program_id(1)
    page = LANES
    d = q_ref.shape[2]
    n_rows = (d // HEAD_DIM) * n_new
    own_head = (_iota((n_rows, d), 0) // n_new) == (_iota((n_rows, d), 1) // HEAD_DIM)
    r2 = _iota((page, 2 * page), 0)
    c2 = _iota((page, 2 * page), 1)
    suffix = jnp.where((c2 >= page) | (r2 > c2), 1.0, 0.0).astype(BF16)

    def visit(scores, values, n_blk, valid):
        nz = scores + jnp.concatenate([biasc_ref[...]] * n_blk, axis=1)
        l1m = _log2_survive(nz)
        if valid is not None:
            l1m = jnp.where(valid, l1m, 0.0)
        blks = [slice(i * page, (i + 1) * page) for i in range(n_blk)]
        cs = [jnp.dot(l1m[:, sl].astype(BF16), suffix, preferred_element_type=F32) for sl in blks]
        carry = carry_scr[...]
        ws = [None] * n_blk
        for i in reversed(range(n_blk)):
            ws[i] = jnp.exp2(l1m[:, blks[i]] - nz[:, blks[i]] + cs[i][:, :page] + carry)
            carry = carry + cs[i][:, page:]
        carry_scr[...] = carry
        if valid is not None:
            ws = [jnp.where(valid, ws[0], 0.0)]
        acc = acc_scr[...]
        for i in range(n_blk):
            acc = acc + values(i, ws[i].astype(BF16))
        acc_scr[...] = acc

    @pl.when(j == 0)
    def _():
        rr = _iota((n_rows, SAMPLE_PAD), 0)
        cc = _iota((n_rows, SAMPLE_PAD), 1)
        rep = jnp.where((rr % n_new) == cc, 1.0, 0.0).astype(BF16)
        qrep = jnp.dot(rep, (q_ref[0] * (-SB_SCALE * LOG2E)).astype(BF16), preferred_element_type=F32)
        qrep_scr[...] = jnp.where(own_head, qrep, 0.0).astype(BF16)
        kpad_scr[...] = jnp.zeros_like(kpad_scr)
        vpad_scr[...] = jnp.zeros_like(vpad_scr)
        kpad_scr[0:SAMPLE_PAD, :] = kn_ref[0].astype(BF16)
        vpad_scr[0:SAMPLE_PAD, :] = vn_ref[0].astype(BF16)
        acc_scr[...] = jnp.zeros_like(acc_scr)
        carry_scr[...] = jnp.zeros_like(carry_scr)
        valid = _iota((n_rows, page), 1) < (_iota((n_rows, page), 0) % n_new)
        scores = lax.dot_general(qrep_scr[...], kpad_scr[...], NT, preferred_element_type=F32)
        visit(scores, lambda i, w: jnp.dot(w, vpad_scr[...], preferred_element_type=F32), 1, valid)

    @pl.when(j > 0)
    def _():
        qrep = qrep_scr[...]
        scores = jnp.concatenate([jnp.dot(qrep, kc_refs[i][0].astype(BF16), preferred_element_type=F32)
                                  for i in range(n_pg)], axis=1)

        def values(i, w):
            return lax.dot_general(w, vc_refs[i][0].astype(BF16), NT, preferred_element_type=F32)

        visit(scores, values, n_pg, None)

    @pl.when(j == n_steps - 1)
    def _():
        rr = _iota((SAMPLE_PAD, n_rows), 0)
        cc = _iota((SAMPLE_PAD, n_rows), 1)
        pick = jnp.where((cc % n_new) == rr, 1.0, 0.0)
        own = jnp.where(own_head, acc_scr[...], 0.0)
        o_ref[0] = jnp.dot(pick, own, preferred_element_type=F32, precision=HI)


def _attn_sample(q, k_new, v_new, cache_k, cache_v, page_table, bias_rows, n_new):
    n_seq, pad, d = q.shape
    n_pages = page_table.shape[1]
    page = cache_k.shape[2]
    assert page == LANES
    n_pg = min(PAGES_PER_STEP, n_pages)
    assert n_pages % n_pg == 0
    n_steps = n_pages // n_pg + 1

    def page_spec(i):
        def index(bi, j, pt_ref):
            return (pt_ref[bi, n_pages - n_pg * jnp.maximum(j, 1) + i], 0, 0)
        return pl.BlockSpec((1, d, page), index)

    row = pl.BlockSpec((1, pad, d), lambda bi, j, pt_ref: (bi, 0, 0))
    pages = [page_spec(i) for i in range(n_pg)]
    n_rows = bias_rows.shape[0]
    return pl.pallas_call(
        functools.partial(_attn_sample_kernel, n_new=n_new, n_steps=n_steps, n_pg=n_pg),
        grid_spec=pltpu.PrefetchScalarGridSpec(
            num_scalar_prefetch=1,
            grid=(n_seq, n_steps),
            in_specs=[row, row, row] + pages + pages
                     + [pl.BlockSpec((n_rows, LANES), lambda bi, j, pt_ref: (0, 0))],
            out_specs=row,
            scratch_shapes=[
                pltpu.VMEM((n_rows, d), BF16),
                pltpu.VMEM((page, d), BF16),
                pltpu.VMEM((page, d), BF16),
                pltpu.VMEM((n_rows, d), F32),
                pltpu.VMEM((n_rows, LANES), F32),
            ],
        ),
        out_shape=jax.ShapeDtypeStruct((n_seq, pad, d), F32),
        compiler_params=_params(("parallel", "arbitrary")),
        name="attn_sample",
    )(page_table, q, k_new, v_new, *([cache_k] * n_pg), *([cache_v] * n_pg), bias_rows)


def kernel(x_prompt, x_sample, cache_k, cache_v, state_wkv, state_shift, page_table, ln_g, ln_b, ffn_w_gate, ffn_w_up, ffn_w_down, tm_mu, tm_w0, tm_w1, tm_w2, tm_a0, tm_a1, tm_a2, tm_v0, tm_v1, tm_v2, tm_g1, tm_g2, tm_k_k, tm_k_a, tm_r_k, tm_w_r, tm_w_k, tm_w_v, tm_w_o, tm_gn_g, tm_gn_b, sb_w_k, sb_w_v, sb_w_q, sb_bias, sb_w_o):
    bp, sp, d = x_prompt.shape
    bs, ss, _ = x_sample.shape
    depth = ln_g.shape[0]
    n_a = tm_mu.shape[0]
    n_heads = d // HEAD_DIM
    n_ff = ffn_w_gate.shape[-1] // FFN_CHUNK
    alpha = (2 * depth) ** 0.25

    def vec(a):
        return a.reshape(1, -1).astype(F32)

    def bf(a):
        return a.astype(BF16)

    hid = jnp.arange(d) // HEAD_DIM
    sel = (hid[:, None] == jnp.arange(LANES)[None, :]).astype(BF16)
    sel_t = sel.T

    ffn_w = {}
    for li in range(depth):
        for j in range(2):
            ffn_w[li, j] = (
                bf(ffn_w_gate[li, j]).reshape(d, n_ff, FFN_CHUNK).transpose(1, 0, 2),
                bf(ffn_w_up[li, j]).reshape(d, n_ff, FFN_CHUNK).transpose(1, 0, 2),
                bf(ffn_w_down[li, j]).reshape(n_ff, FFN_CHUNK, d),
                vec(ln_g[li, 2 * j]), vec(ln_b[li, 2 * j]))
    pre_w, post_w = [], []
    for li in range(n_a):
        w = [tm_mu[li], bf(tm_w_r[li]), bf(tm_w_k[li]), bf(tm_w_v[li]),
             vec(tm_w0[li]), bf(tm_w1[li]), bf(tm_w2[li]), vec(tm_a0[li]), bf(tm_a1[li]), bf(tm_a2[li])]
        if li > 0:
            w += [vec(tm_v0[li - 1]), bf(tm_v1[li - 1]), bf(tm_v2[li - 1])]
        w += [bf(tm_g1[li]), bf(tm_g2[li]), vec(tm_k_k[li]), vec(tm_k_a[li]), sel, sel_t]
        pre_w.append(w)
        post_w.append([bf(tm_w_o[li]), vec(tm_gn_g[li]), vec(tm_gn_b[li]), vec(tm_r_k[li]),
                       vec(ln_g[li, 1]), vec(ln_b[li, 1]), sel, sel_t])
    wq = [bf(sb_w_q[j]) for j in range(depth - n_a)]
    wo = [bf(sb_w_o[j]) for j in range(depth - n_a)]
    wkv = [bf(sb_w_k), bf(sb_w_v)]
    n_phys, page = cache_k.shape[0], cache_k.shape[1]
    cache_k2 = jnp.transpose(cache_k, (0, 2, 3, 1)).reshape(n_phys, d, page)
    cache_v2 = jnp.transpose(cache_v, (0, 2, 3, 1)).reshape(n_phys, d, page)

    def trunk(x, n_seq, t, wkv_in, shift_in):
        fresh = shift_in is None
        n = n_seq * t
        new_wkv, new_shift = [], []
        v_first = None
        k_sh = v_sh = None

        def pad_rows(a):
            a = a.reshape(n_seq, t, d)
            return jnp.pad(a, ((0, 0), (0, SAMPLE_PAD - t), (0, 0)))

        for li in range(depth):
            x = _ffn_half(x, *ffn_w[li, 0], alpha)
            if li < n_a:
                x3 = x.reshape(n_seq, t, d)
                new_shift.append(x3[:, -1])
                if fresh:
                    x_shift = None
                else:
                    x_shift = jnp.concatenate([shift_in[li][:, None, :], x3[:, :-1]], axis=1).reshape(n, d)
                r, lw, k, v, a, b, g = _tm_pre(x, x_shift, v_first if li > 0 else None, pre_w[li], t)
                if li == 0:
                    v_first = v
                if fresh:
                    y, s_new = _rwkv_scan(r, lw, k, v, a, b, wkv_in[li], SCAN_CHUNK)
                else:
                    padded = (pad_rows(z).reshape(n_seq * SAMPLE_PAD, d) for z in (r, lw, k, v, a, b))
                    y, s_new = _rwkv_scan(*padded, wkv_in[li], SAMPLE_PAD)
                    y = y.reshape(n_seq, SAMPLE_PAD, d)[:, :t].reshape(n, d)
                new_wkv.append(s_new)
                x = _tm_post(y, r, k, v, g, x, post_w[li], alpha)
            else:
                j = li - n_a
                (q,) = _proj(x, [wq[j]], "q_proj")
                if fresh:
                    o = _attn_prompt(q, k_sh, v_sh, sb_bias[j].astype(F32), n_seq)
                else:
                    nbias2 = sb_bias[j].astype(F32) * (-LOG2E)
                    bias_rows = jnp.broadcast_to(jnp.repeat(nbias2, t)[:, None], (n_heads * t, LANES))
                    o = _attn_sample(pad_rows(q), pad_rows(k_sh), pad_rows(v_sh), cache_k2, cache_v2,
                                     page_table, bias_rows, t)
                    o = o[:, :t].reshape(n, d)
                x = _oproj_postnorm(o, x, wo[j], vec(ln_g[li, 1]), vec(ln_b[li, 1]), alpha)
            x = _ffn_half(x, *ffn_w[li, 1], alpha)
            if li == n_a - 1:
                k_sh, v_sh = _proj(x, wkv, "kv_proj")
        return x, jnp.stack(new_wkv), jnp.stack(new_shift), k_sh, v_sh

    zero_state = jnp.zeros((n_a, bp, n_heads, HEAD_DIM, HEAD_DIM), F32)
    y_p, wkv_p, shift_p, k_p, v_p = trunk(x_prompt.reshape(bp * sp, d), bp, sp, zero_state, None)
    y_s, wkv_s, shift_s, k_s, v_s = trunk(x_sample.reshape(bs * ss, d), bs, ss, state_wkv, state_shift)

    def paged(a):
        return a.reshape(bp, sp // PAGE_SIZE, PAGE_SIZE, n_heads, HEAD_DIM)

    return (y_p.reshape(bp, sp, d), y_s.reshape(bs, ss, d), wkv_p, shift_p, paged(k_p), paged(v_p),
            wkv_s, shift_s, k_s.reshape(bs, ss, n_heads, HEAD_DIM), v_s.reshape(bs, ss, n_heads, HEAD_DIM))
```

```python
import functools

import jax
import jax.numpy as jnp
from jax import lax
from jax.experimental import pallas as pl
from jax.experimental.pallas import tpu as pltpu

HEAD_DIM = 64
LANES = 128
SUBLANES = 8
MXU_W = 256
LN_EPS = 1e-5
GN_EPS = 64e-5
SB_SCALE = HEAD_DIM ** -0.5
PAGE_SIZE = 128

FFN_CHUNK = MXU_W
TOKEN_TILE = 512
PRE_TILE = 256
SCAN_CHUNK = 64
SAMPLE_PAD = SUBLANES
ATT_Q = 256
ATT_K = LANES
ATT_SUB = 16
PAGES_PER_STEP = 8
VMEM_LIMIT = 56 * 1024 * 1024

F32 = jnp.float32
BF16 = jnp.bfloat16
HI = lax.Precision.HIGHEST
NT = (((1,), (1,)), ((), ()))


def _const_spec(shape):
    nd = len(shape)
    return pl.BlockSpec(shape, lambda *_: (0,) * nd, pipeline_mode=pl.Buffered(1))


def _params(sem):
    return pltpu.CompilerParams(dimension_semantics=sem, vmem_limit_bytes=VMEM_LIMIT)


def _iota(shape, axis):
    return lax.broadcasted_iota(jnp.int32, shape, axis)


def _layer_norm(y, g, b):
    mu = jnp.mean(y, axis=-1, keepdims=True)
    yc = y - mu
    var = jnp.mean(yc * yc, axis=-1, keepdims=True)
    return yc * lax.rsqrt(var + LN_EPS) * g + b


def _sigmoid(x):
    return 1.0 / (1.0 + jnp.exp(-x))


def _split_bf16(x):
    hi = x.astype(BF16)
    lo = (x - hi.astype(F32)).astype(BF16)
    return hi, lo


def _mm(a, b):
    return jnp.dot(a.astype(BF16), b.astype(BF16), preferred_element_type=F32)


def _mm_nt(a, b):
    return lax.dot_general(a.astype(BF16), b.astype(BF16), NT, preferred_element_type=F32)


def _head_sum(q, sel, sel_t):
    hi, lo = _split_bf16(q)
    s = jnp.dot(hi, sel, preferred_element_type=F32) + jnp.dot(lo, sel, preferred_element_type=F32)
    hi, lo = _split_bf16(s)
    return jnp.dot(hi, sel_t, preferred_element_type=F32) + jnp.dot(lo, sel_t, preferred_element_type=F32)


def _ffn_kernel(x_ref, wg_ref, wu_ref, wd_ref, g_ref, b_ref, o_ref, acc_ref, *, alpha, n_chunks):
    x = x_ref[...]
    xb = x.astype(BF16)
    acc_ref[...] = jnp.zeros_like(acc_ref)

    def body(c, carry):
        gate = jnp.dot(xb, wg_ref[c], preferred_element_type=F32)
        up = jnp.dot(xb, wu_ref[c], preferred_element_type=F32)
        h = (gate * _sigmoid(gate) * up).astype(BF16)
        acc_ref[...] += jnp.dot(h, wd_ref[c], preferred_element_type=F32)
        return carry

    lax.fori_loop(0, n_chunks, body, 0, unroll=True)
    y = alpha * x + 0.5 * acc_ref[...]
    o_ref[...] = _layer_norm(y, g_ref[...], b_ref[...])


def _ffn_half(x, wg, wu, wd, g, b, alpha):
    n, d = x.shape
    n_chunks = wg.shape[0]
    tm = TOKEN_TILE
    kern = functools.partial(_ffn_kernel, alpha=alpha, n_chunks=n_chunks)
    return pl.pallas_call(
        kern,
        grid=(n // tm,),
        in_specs=[
            pl.BlockSpec((tm, d), lambda i: (i, 0)),
            _const_spec(wg.shape), _const_spec(wu.shape), _const_spec(wd.shape),
            _const_spec(g.shape), _const_spec(b.shape),
        ],
        out_specs=pl.BlockSpec((tm, d), lambda i: (i, 0)),
        out_shape=jax.ShapeDtypeStruct((n, d), F32),
        scratch_shapes=[pltpu.VMEM((tm, d), F32)],
        compiler_params=_params(("parallel",)),
        name="ffn_half",
    )(x, wg, wu, wd, g, b)


def _tm_pre_kernel(*refs, has_vres, seq_len, tile):
    x_ref, shift_ref = refs[0], refs[1]
    refs = refs[2:]
    if has_vres:
        vf_ref, refs = refs[0], refs[1:]
    (mu_ref, wr_ref, wk_ref, wv_ref, w0_ref, w1_ref, w2_ref, a0_ref, a1_ref, a2_ref) = refs[:10]
    refs = refs[10:]
    if has_vres:
        (v0_ref, v1_ref, v2_ref), refs = refs[:3], refs[3:]
    (g1_ref, g2_ref, kk_ref, ka_ref, sel_ref, selt_ref,
     r_out, lw_out, k_out, v_out, a_out, b_out, g_out) = refs
    x = x_ref[...]
    if seq_len is None:
        xs = shift_ref[...]
    else:
        starts_sequence = (pl.program_id(0) * tile) % seq_len == 0
        first = jnp.where(starts_sequence, 0.0, shift_ref[SUBLANES - 1:SUBLANES, :])
        xs = jnp.where(_iota(x.shape, 0) == 0, first, pltpu.roll(x, 1, 0))
    xx = xs - x

    def mix(i):
        return (x + xx * mu_ref[i:i + 1, :]).astype(BF16)

    def mm(a, w_ref):
        return jnp.dot(a, w_ref[...], preferred_element_type=F32)

    xr, xw, xk, xv, xa, xg = (mix(i) for i in range(6))
    r_out[...] = mm(xr, wr_ref)
    w_in = w0_ref[...] + mm(jnp.tanh(mm(xw, w1_ref)).astype(BF16), w2_ref)
    softplus = jnp.maximum(-w_in, 0.0) + jnp.log1p(jnp.exp(-jnp.abs(w_in)))
    lw_out[...] = -jnp.exp(-softplus - 0.5)
    k = mm(xk, wk_ref)
    v = mm(xv, wv_ref)
    if has_vres:
        gate = _sigmoid(v0_ref[...] + mm(mm(xv, v1_ref).astype(BF16), v2_ref))
        v = v + (vf_ref[...] - v) * gate
    v_out[...] = v
    a = _sigmoid(a0_ref[...] + mm(mm(xa, a1_ref).astype(BF16), a2_ref))
    g_out[...] = mm(_sigmoid(mm(xg, g1_ref)).astype(BF16), g2_ref)
    kk = k * kk_ref[...]
    norm = jnp.sqrt(_head_sum(kk * kk, sel_ref[...], selt_ref[...]))
    kk = kk / jnp.maximum(norm, 1e-12)
    k_out[...] = k * (1.0 + (a - 1.0) * ka_ref[...])
    a_out[...] = -kk
    b_out[...] = kk * a


def _tm_pre(x, x_shift, v_first, weights, seq_len):
    n, d = x.shape
    tm = PRE_TILE
    has_vres = v_first is not None
    tile = pl.BlockSpec((tm, d), lambda i: (i, 0))
    if x_shift is None:
        per = tm // SUBLANES
        shift_in = x
        shift_spec = pl.BlockSpec((SUBLANES, d), lambda i: (jnp.maximum(i * per - 1, 0), 0))
    else:
        seq_len = None
        shift_in = x_shift
        shift_spec = tile
    acts = [x, shift_in] + ([v_first] if has_vres else [])
    kern = functools.partial(_tm_pre_kernel, has_vres=has_vres, seq_len=seq_len, tile=tm)
    out = jax.ShapeDtypeStruct((n, d), F32)
    return pl.pallas_call(
        kern,
        grid=(n // tm,),
        in_specs=[tile, shift_spec] + [tile] * has_vres + [_const_spec(w.shape) for w in weights],
        out_specs=[tile] * 7,
        out_shape=[out] * 7,
        compiler_params=_params(("parallel",)),
        name="tm_pre_vres" if has_vres else "tm_pre",
    )(*acts, *weights)


def _scan_kernel(r_ref, lw_ref, k_ref, v_ref, a_ref, b_ref, s0_ref, y_ref, sout_ref, s_scr,
                 *, chunk, n_chunks, n_groups):
    c = pl.program_id(1)
    hd = HEAD_DIM
    gw = MXU_W
    hpg = gw // hd
    pw = hpg * chunk

    @pl.when(c == 0)
    def _():
        s_scr[...] = jnp.zeros_like(s_scr)
        for g in range(n_groups):
            for h in range(hpg):
                s_scr[g, h * hd:(h + 1) * hd, h * hd:(h + 1) * hd] = s0_ref[0, g * hpg + h]

    row_p = _iota((chunk, pw), 0)
    j_p = _iota((chunk, pw), 1) % chunk
    strict = j_p < row_p
    incl = j_p <= row_p
    tri = jnp.where(_iota((chunk, chunk), 1) <= _iota((chunk, chunk), 0), 1.0, 0.0).astype(BF16)
    bd_pp = (_iota((pw, pw), 0) // chunk) == (_iota((pw, pw), 1) // chunk)
    bd_pg = (_iota((pw, gw), 0) // chunk) == (_iota((pw, gw), 1) // hd)
    bd_gg = (_iota((gw, gw), 0) // hd) == (_iota((gw, gw), 1) // hd)

    def heads_on_rows(x, mask):
        return jnp.where(mask, jnp.concatenate([x] * hpg, axis=0), 0.0).astype(BF16)

    groups = range(n_groups)
    sls = [slice(g * gw, (g + 1) * gw) for g in groups]
    vs, kbs, w_lasts, ars, r_bs, r_ks = [], [], [], [], [], []
    for sl in sls:
        r, lw, k, v, a, b = (ref[:, sl] for ref in (r_ref, lw_ref, k_ref, v_ref, a_ref, b_ref))
        hi = lw.astype(BF16)
        rem = lw - hi.astype(F32)
        mid = rem.astype(BF16)
        lo = (rem - mid.astype(F32)).astype(BF16)
        cum = (jnp.dot(tri, hi, preferred_element_type=F32) + jnp.dot(tri, mid, preferred_element_type=F32)
               + jnp.dot(tri, lo, preferred_element_type=F32))
        w_last = cum[chunk - 1:chunk, :]
        w_neg = jnp.exp(-cum)
        w_rest = jnp.exp(w_last - cum)
        vs.append(v)
        w_lasts.append(w_last)
        kbs.append(jnp.concatenate([k * w_rest, b * w_rest], axis=0))
        ars.append(jnp.concatenate([a * jnp.exp(cum - lw), r * jnp.exp(cum)], axis=0).astype(BF16))
        r_bs.append(heads_on_rows(b * w_neg, bd_pg))
        r_ks.append(heads_on_rows(k * w_neg, bd_pg))

    ss = [s_scr[g] for g in groups]
    g_bs = [_mm_nt(ars[g], r_bs[g]) for g in groups]
    g_ks = [_mm_nt(ars[g], r_ks[g]) for g in groups]
    from_state = [_mm_nt(ars[g], ss[g]) for g in groups]
    v_bds = [heads_on_rows(vs[g], bd_pg) for g in groups]
    a_abs = [jnp.where(strict, g_bs[g][:chunk], 0.0) for g in groups]
    a_rbs = [jnp.where(incl, g_bs[g][chunk:], 0.0) for g in groups]
    a_aks = [jnp.where(strict, g_ks[g][:chunk], 0.0) for g in groups]
    a_rks = [jnp.where(incl, g_ks[g][chunk:], 0.0) for g in groups]
    us = [from_state[g][:chunk] + _mm(a_aks[g], v_bds[g]) for g in groups]
    powers = a_abs
    span = 1
    while True:
        us = [us[g] + _mm(powers[g], heads_on_rows(us[g], bd_pg)) for g in groups]
        span *= 2
        if span >= chunk:
            break
        powers = [_mm(powers[g], heads_on_rows(powers[g], bd_pp)) for g in groups]
    for g in groups:
        y_ref[:, sls[g]] = (from_state[g][chunk:] + _mm(a_rks[g], v_bds[g])
                            + _mm(a_rbs[g], heads_on_rows(us[g], bd_pg)))
    for g in groups:
        vu_t = jnp.transpose(jnp.concatenate([vs[g], us[g]], axis=0))
        s_scr[g] = ss[g] * jnp.exp(w_lasts[g]) + jnp.where(bd_gg, _mm(vu_t, kbs[g]), 0.0)

    @pl.when(c == n_chunks - 1)
    def _():
        for g in range(n_groups):
            for h in range(hpg):
                sout_ref[0, g * hpg + h] = s_scr[g, h * hd:(h + 1) * hd, h * hd:(h + 1) * hd]


def _rwkv_scan(r, lw, k, v, a, b, s0, chunk):
    n, d = r.shape
    n_seq, n_heads = s0.shape[0], s0.shape[1]
    t = n // n_seq
    n_chunks = t // chunk
    n_groups = d // MXU_W
    tile = pl.BlockSpec((chunk, d), lambda bi, c: (bi * n_chunks + c, 0))
    st = pl.BlockSpec((1, n_heads, HEAD_DIM, HEAD_DIM), lambda bi, c: (bi, 0, 0, 0))
    kern = functools.partial(_scan_kernel, chunk=chunk, n_chunks=n_chunks, n_groups=n_groups)
    return pl.pallas_call(
        kern,
        grid=(n_seq, n_chunks),
        in_specs=[tile] * 6 + [st],
        out_specs=[tile, st],
        out_shape=[jax.ShapeDtypeStruct((n, d), F32), jax.ShapeDtypeStruct(s0.shape, F32)],
        scratch_shapes=[pltpu.VMEM((n_groups, MXU_W, MXU_W), F32)],
        compiler_params=_params(("parallel", "arbitrary")),
        name=f"rwkv_scan_c{chunk}",
    )(r, lw, k, v, a, b, s0)


def _tm_post_kernel(y_ref, r_ref, k_ref, v_ref, g_ref, x_ref, wo_ref, gng_ref, gnb_ref, rk_ref,
                    lng_ref, lnb_ref, sel_ref, selt_ref, o_ref, *, alpha):
    sel, selt = sel_ref[...], selt_ref[...]
    y = y_ref[...]
    inv = 1.0 / HEAD_DIM
    yc = y - _head_sum(y, sel, selt) * inv
    var = _head_sum(yc * yc, sel, selt) * inv
    yn = yc * lax.rsqrt(var + GN_EPS) * gng_ref[...] + gnb_ref[...]
    bonus = _head_sum(r_ref[...] * k_ref[...] * rk_ref[...], sel, selt) * v_ref[...]
    z = ((yn + bonus) * g_ref[...]).astype(BF16)
    out = jnp.dot(z, wo_ref[...], preferred_element_type=F32)
    o_ref[...] = _layer_norm(alpha * x_ref[...] + out, lng_ref[...], lnb_ref[...])


def _tm_post(y, r, k, v, g, x, weights, alpha):
    n, d = x.shape
    tm = PRE_TILE
    tile = pl.BlockSpec((tm, d), lambda i: (i, 0))
    return pl.pallas_call(
        functools.partial(_tm_post_kernel, alpha=alpha),
        grid=(n // tm,),
        in_specs=[tile] * 6 + [_const_spec(w.shape) for w in weights],
        out_specs=tile,
        out_shape=jax.ShapeDtypeStruct((n, d), F32),
        compiler_params=_params(("parallel",)),
        name="tm_post",
    )(y, r, k, v, g, x, *weights)


def _proj_kernel(x_ref, *refs):
    n_out = len(refs) // 2
    xb = x_ref[...].astype(BF16)
    for w_ref, o_ref in zip(refs[:n_out], refs[n_out:]):
        o_ref[...] = jnp.dot(xb, w_ref[...], preferred_element_type=F32)


def _proj(x, ws, name):
    n, d = x.shape
    tm = TOKEN_TILE
    tile = pl.BlockSpec((tm, d), lambda i: (i, 0))
    return pl.pallas_call(
        _proj_kernel,
        grid=(n // tm,),
        in_specs=[tile] + [_const_spec(w.shape) for w in ws],
        out_specs=[pl.BlockSpec((tm, w.shape[1]), lambda i: (i, 0)) for w in ws],
        out_shape=[jax.ShapeDtypeStruct((n, w.shape[1]), F32) for w in ws],
        compiler_params=_params(("parallel",)),
        name=name,
    )(x, *ws)


def _oproj_kernel(o_ref, x_ref, w_ref, g_ref, b_ref, out_ref, *, alpha):
    sub = jnp.dot(o_ref[...].astype(BF16), w_ref[...], preferred_element_type=F32)
    out_ref[...] = _layer_norm(alpha * x_ref[...] + sub, g_ref[...], b_ref[...])


def _oproj_postnorm(o, x, w, g, b, alpha):
    n, d = x.shape
    tm = TOKEN_TILE
    tile = pl.BlockSpec((tm, d), lambda i: (i, 0))
    return pl.pallas_call(
        functools.partial(_oproj_kernel, alpha=alpha),
        grid=(n // tm,),
        in_specs=[tile, tile, _const_spec(w.shape), _const_spec(g.shape), _const_spec(b.shape)],
        out_specs=tile,
        out_shape=jax.ShapeDtypeStruct((n, d), F32),
        compiler_params=_params(("parallel",)),
        name="oproj_postnorm",
    )(o, x, w, g, b)


LOG2E = 1.4426950408889634


def _log2_survive(nz2):
    sign = jnp.uint32(0x80000000)
    neg_abs = lax.bitcast_convert_type(lax.bitcast_convert_type(nz2, jnp.uint32) | sign, F32)
    return jnp.minimum(nz2, 0.0) - jnp.log2(1.0 + jnp.exp2(neg_abs))


def _attn_prompt_kernel(bias_ref, q_ref, k_ref, v_ref, o_ref, *, tq, tk, n_sub):
    p = pl.program_id(1)
    qb = pl.program_id(2)
    hd = HEAD_DIM
    q = (q_ref[...] * (-SB_SCALE * LOG2E)).astype(BF16)
    bias = jnp.where(_iota((1, 2 * tk), 1) < tk, bias_ref[2 * p], bias_ref[2 * p + 1]) * (-LOG2E)
    row = _iota((tq, tk), 0)
    col = _iota((tq, tk), 1)
    head0_k = _iota((tk, LANES), 1) < hd
    r2 = _iota((tk, 2 * tk), 0)
    c2 = _iota((tk, 2 * tk), 1)
    suffix = jnp.where((c2 >= tk) | (r2 > c2), 1.0, 0.0).astype(BF16)

    kpi = n_sub * tk
    per_q = tq // tk
    n_phase = kpi // tq
    diag = (qb * tq) // kpi

    def heads_on_rows(blk):
        return jnp.concatenate([jnp.where(head0_k, blk, 0.0), jnp.where(head0_k, 0.0, blk)], axis=0).astype(BF16)

    def key_blocks(start, carry, phase):
        acc, car0, car1 = carry
        n_act = n_sub if phase is None else (phase + 1) * per_q
        subs = list(reversed(range(n_act)))
        st = [dict() for _ in subs]

        def stage(i, s):
            sub, c = subs[i], st[i]
            masked = phase is not None and sub >= phase * per_q
            if s == 0:
                kbd = heads_on_rows(k_ref[pl.ds(start + sub * tk, tk), :])
                c["nz"] = lax.dot_general(q, kbd, NT, preferred_element_type=F32) + bias
            elif s == 1:
                l1m = _log2_survive(c["nz"])
                if masked:
                    c["valid"] = jnp.concatenate([(col + (sub - phase * per_q) * tk) < row] * 2, axis=1)
                    l1m = jnp.where(c["valid"], l1m, 0.0)
                c["l1m"] = l1m
            elif s == 2:
                l1m = c["l1m"].astype(BF16)
                c["cs"] = [jnp.dot(l1m[:, h * tk:(h + 1) * tk], suffix, preferred_element_type=F32)
                           for h in range(2)]
            elif s == 3:
                nonlocal car0, car1
                surv = jnp.concatenate([c["cs"][0][:, :tk] + car0, c["cs"][1][:, :tk] + car1], axis=1)
                w = jnp.exp2(c["l1m"] - c["nz"] + surv)
                if masked:
                    w = jnp.where(c["valid"], w, 0.0)
                c["w"] = w.astype(BF16)
                car0 = car0 + c["cs"][0][:, tk:]
                car1 = car1 + c["cs"][1][:, tk:]
            else:
                nonlocal acc
                vbd = heads_on_rows(v_ref[pl.ds(start + sub * tk, tk), :])
                acc = acc + jnp.dot(c["w"], vbd, preferred_element_type=F32)

        n_stages = 5
        for t in range(n_act + n_stages - 1):
            for i in range(n_act):
                if 0 <= t - i < n_stages:
                    stage(i, t - i)
        return acc, car0, car1

    zero = jnp.zeros((tq, LANES), F32)
    if kpi == k_ref.shape[0]:
        carry = lax.switch(qb, [functools.partial(key_blocks, 0, (zero, zero, zero), ph)
                                for ph in range(n_phase)])
    else:
        diag_start = pl.multiple_of(diag * kpi, kpi)
        carry = lax.switch(qb % n_phase,
                           [functools.partial(key_blocks, diag_start, (zero, zero, zero), ph)
                            for ph in range(n_phase)])

        def body(i, carry):
            return key_blocks(pl.multiple_of((diag - i) * kpi, kpi), carry, None)

        carry = lax.fori_loop(1, diag + 1, body, carry)
    o_ref[...] = carry[0]


def _attn_prompt(q, k, v, bias, n_seq):
    n, d = q.shape
    t = n // n_seq
    tq, tk = ATT_Q, ATT_K
    n_sub = min(ATT_SUB, t // tk)
    assert tk == LANES and tq % tk == 0 and (n_sub * tk) % tq == 0 and t % (n_sub * tk) == 0
    n_blk = t // tq
    n_pairs = d // LANES
    return pl.pallas_call(
        functools.partial(_attn_prompt_kernel, tq=tq, tk=tk, n_sub=n_sub),
        grid_spec=pltpu.PrefetchScalarGridSpec(
            num_scalar_prefetch=1,
            grid=(n_seq, n_pairs, n_blk),
            in_specs=[
                pl.BlockSpec((tq, LANES), lambda bi, p, qb, bias_ref: (bi * n_blk + qb, p)),
                pl.BlockSpec((t, LANES), lambda bi, p, qb, bias_ref: (bi, p)),
                pl.BlockSpec((t, LANES), lambda bi, p, qb, bias_ref: (bi, p)),
            ],
            out_specs=pl.BlockSpec((tq, LANES), lambda bi, p, qb, bias_ref: (bi * n_blk + qb, p)),
        ),
        out_shape=jax.ShapeDtypeStruct((n, d), F32),
        compiler_params=_params(("parallel", "parallel", "arbitrary")),
        name="attn_prompt",
    )(bias, q, k, v)


def _attn_sample_kernel(pt_ref, q_ref, kn_ref, vn_ref, *refs, n_new, n_steps, n_pg):
    kc_refs, vc_refs = refs[:n_pg], refs[n_pg:2 * n_pg]
    biasc_ref, o_ref, qrep_scr, kpad_scr, vpad_scr, acc_scr, carry_scr = refs[2 * n_pg:]
    j = pl.program_id(1)
    page = LANES
    d = q_ref.shape[2]
    n_rows = (d // HEAD_DIM) * n_new
    own_head = (_iota((n_rows, d), 0) // n_new) == (_iota((n_rows, d), 1) // HEAD_DIM)
    r2 = _iota((page, 2 * page), 0)
    c2 = _iota((page, 2 * page), 1)
    suffix = jnp.where((c2 >= page) | (r2 > c2), 1.0, 0.0).astype(BF16)

    def visit(scores, values, n_blk, valid):
        nz = scores + jnp.concatenate([biasc_ref[...]] * n_blk, axis=1)
        l1m = _log2_survive(nz)
        if valid is not None:
            l1m = jnp.where(valid, l1m, 0.0)
        blks = [slice(i * page, (i + 1) * page) for i in range(n_blk)]
        cs = [jnp.dot(l1m[:, sl].astype(BF16), suffix, preferred_element_type=F32) for sl in blks]
        carry = carry_scr[...]
        ws = [None] * n_blk
        for i in reversed(range(n_blk)):
            ws[i] = jnp.exp2(l1m[:, blks[i]] - nz[:, blks[i]] + cs[i][:, :page] + carry)
            carry = carry + cs[i][:, page:]
        carry_scr[...] = carry
        if valid is not None:
            ws = [jnp.where(valid, ws[0], 0.0)]
        acc = acc_scr[...]
        for i in range(n_blk):
            acc = acc + values(i, ws[i].astype(BF16))
        acc_scr[...] = acc

    @pl.when(j == 0)
    def _():
        rr = _iota((n_rows, SAMPLE_PAD), 0)
        cc = _iota((n_rows, SAMPLE_PAD), 1)
        rep = jnp.where((rr % n_new) == cc, 1.0, 0.0).astype(BF16)
        qrep = jnp.dot(rep, (q_ref[0] * (-SB_SCALE * LOG2E)).astype(BF16), preferred_element_type=F32)
        qrep_scr[...] = jnp.where(own_head, qrep, 0.0).astype(BF16)
        kpad_scr[...] = jnp.zeros_like(kpad_scr)
        vpad_scr[...] = jnp.zeros_like(vpad_scr)
        kpad_scr[0:SAMPLE_PAD, :] = kn_ref[0].astype(BF16)
        vpad_scr[0:SAMPLE_PAD, :] = vn_ref[0].astype(BF16)
        acc_scr[...] = jnp.zeros_like(acc_scr)
        carry_scr[...] = jnp.zeros_like(carry_scr)
        valid = _iota((n_rows, page), 1) < (_iota((n_rows, page), 0) % n_new)
        scores = lax.dot_general(qrep_scr[...], kpad_scr[...], NT, preferred_element_type=F32)
        visit(scores, lambda i, w: jnp.dot(w, vpad_scr[...], preferred_element_type=F32), 1, valid)

    @pl.when(j > 0)
    def _():
        qrep = qrep_scr[...]
        scores = jnp.concatenate([jnp.dot(qrep, kc_refs[i][0].astype(BF16), preferred_element_type=F32)
                                  for i in range(n_pg)], axis=1)

        def values(i, w):
            return lax.dot_general(w, vc_refs[i][0].astype(BF16), NT, preferred_element_type=F32)

        visit(scores, values, n_pg, None)

    @pl.when(j == n_steps - 1)
    def _():
        rr = _iota((SAMPLE_PAD, n_rows), 0)
        cc = _iota((SAMPLE_PAD, n_rows), 1)
        pick = jnp.where((cc % n_new) == rr, 1.0, 0.0)
        own = jnp.where(own_head, acc_scr[...], 0.0)
        o_ref[0] = jnp.dot(pick, own, preferred_element_type=F32, precision=HI)


def _attn_sample(q, k_new, v_new, cache_k, cache_v, page_table, bias_rows, n_new):
    n_seq, pad, d = q.shape
    n_pages = page_table.shape[1]
    page = cache_k.shape[2]
    assert page == LANES
    n_pg = min(PAGES_PER_STEP, n_pages)
    assert n_pages % n_pg == 0
    n_steps = n_pages // n_pg + 1

    def page_spec(i):
        def index(bi, j, pt_ref):
            return (pt_ref[bi, n_pages - n_pg * jnp.maximum(j, 1) + i], 0, 0)
        return pl.BlockSpec((1, d, page), index)

    row = pl.BlockSpec((1, pad, d), lambda bi, j, pt_ref: (bi, 0, 0))
    pages = [page_spec(i) for i in range(n_pg)]
    n_rows = bias_rows.shape[0]
    return pl.pallas_call(
        functools.partial(_attn_sample_kernel, n_new=n_new, n_steps=n_steps, n_pg=n_pg),
        grid_spec=pltpu.PrefetchScalarGridSpec(
            num_scalar_prefetch=1,
            grid=(n_seq, n_steps),
            in_specs=[row, row, row] + pages + pages
                     + [pl.BlockSpec((n_rows, LANES), lambda bi, j, pt_ref: (0, 0))],
            out_specs=row,
            scratch_shapes=[
                pltpu.VMEM((n_rows, d), BF16),
                pltpu.VMEM((page, d), BF16),
                pltpu.VMEM((page, d), BF16),
                pltpu.VMEM((n_rows, d), F32),
                pltpu.VMEM((n_rows, LANES), F32),
            ],
        ),
        out_shape=jax.ShapeDtypeStruct((n_seq, pad, d), F32),
        compiler_params=_params(("parallel", "arbitrary")),
        name="attn_sample",
    )(page_table, q, k_new, v_new, *([cache_k] * n_pg), *([cache_v] * n_pg), bias_rows)


def kernel(x_prompt, x_sample, cache_k, cache_v, state_wkv, state_shift, page_table, ln_g, ln_b, ffn_w_gate, ffn_w_up, ffn_w_down, tm_mu, tm_w0, tm_w1, tm_w2, tm_a0, tm_a1, tm_a2, tm_v0, tm_v1, tm_v2, tm_g1, tm_g2, tm_k_k, tm_k_a, tm_r_k, tm_w_r, tm_w_k, tm_w_v, tm_w_o, tm_gn_g, tm_gn_b, sb_w_k, sb_w_v, sb_w_q, sb_bias, sb_w_o):
    bp, sp, d = x_prompt.shape
    bs, ss, _ = x_sample.shape
    depth = ln_g.shape[0]
    n_a = tm_mu.shape[0]
    n_heads = d // HEAD_DIM
    n_ff = ffn_w_gate.shape[-1] // FFN_CHUNK
    alpha = (2 * depth) ** 0.25

    def vec(a):
        return a.reshape(1, -1).astype(F32)

    def bf(a):
        return a.astype(BF16)

    hid = jnp.arange(d) // HEAD_DIM
    sel = (hid[:, None] == jnp.arange(LANES)[None, :]).astype(BF16)
    sel_t = sel.T

    ffn_w = {}
    for li in range(depth):
        for j in range(2):
            ffn_w[li, j] = (
                bf(ffn_w_gate[li, j]).reshape(d, n_ff, FFN_CHUNK).transpose(1, 0, 2),
                bf(ffn_w_up[li, j]).reshape(d, n_ff, FFN_CHUNK).transpose(1, 0, 2),
                bf(ffn_w_down[li, j]).reshape(n_ff, FFN_CHUNK, d),
                vec(ln_g[li, 2 * j]), vec(ln_b[li, 2 * j]))
    pre_w, post_w = [], []
    for li in range(n_a):
        w = [tm_mu[li], bf(tm_w_r[li]), bf(tm_w_k[li]), bf(tm_w_v[li]),
             vec(tm_w0[li]), bf(tm_w1[li]), bf(tm_w2[li]), vec(tm_a0[li]), bf(tm_a1[li]), bf(tm_a2[li])]
        if li > 0:
            w += [vec(tm_v0[li - 1]), bf(tm_v1[li - 1]), bf(tm_v2[li - 1])]
        w += [bf(tm_g1[li]), bf(tm_g2[li]), vec(tm_k_k[li]), vec(tm_k_a[li]), sel, sel_t]
        pre_w.append(w)
        post_w.append([bf(tm_w_o[li]), vec(tm_gn_g[li]), vec(tm_gn_b[li]), vec(tm_r_k[li]),
                       vec(ln_g[li, 1]), vec(ln_b[li, 1]), sel, sel_t])
    wq = [bf(sb_w_q[j]) for j in range(depth - n_a)]
    wo = [bf(sb_w_o[j]) for j in range(depth - n_a)]
    wkv = [bf(sb_w_k), bf(sb_w_v)]
    n_phys, page = cache_k.shape[0], cache_k.shape[1]
    cache_k2 = jnp.transpose(cache_k, (0, 2, 3, 1)).reshape(n_phys, d, page)
    cache_v2 = jnp.transpose(cache_v, (0, 2, 3, 1)).reshape(n_phys, d, page)

    def trunk(x, n_seq, t, wkv_in, shift_in):
        fresh = shift_in is None
        n = n_seq * t
        new_wkv, new_shift = [], []
        v_first = None
        k_sh = v_sh = None

        def pad_rows(a):
            a = a.reshape(n_seq, t, d)
            return jnp.pad(a, ((0, 0), (0, SAMPLE_PAD - t), (0, 0)))

        for li in range(depth):
            x = _ffn_half(x, *ffn_w[li, 0], alpha)
            if li < n_a:
                x3 = x.reshape(n_seq, t, d)
                new_shift.append(x3[:, -1])
                if fresh:
                    x_shift = None
                else:
                    x_shift = jnp.concatenate([shift_in[li][:, None, :], x3[:, :-1]], axis=1).reshape(n, d)
                r, lw, k, v, a, b, g = _tm_pre(x, x_shift, v_first if li > 0 else None, pre_w[li], t)
                if li == 0:
                    v_first = v
                if fresh:
                    y, s_new = _rwkv_scan(r, lw, k, v, a, b, wkv_in[li], SCAN_CHUNK)
                else:
                    padded = (pad_rows(z).reshape(n_seq * SAMPLE_PAD, d) for z in (r, lw, k, v, a, b))
                    y, s_new = _rwkv_scan(*padded, wkv_in[li], SAMPLE_PAD)
                    y = y.reshape(n_seq, SAMPLE_PAD, d)[:, :t].reshape(n, d)
                new_wkv.append(s_new)
                x = _tm_post(y, r, k, v, g, x, post_w[li], alpha)
            else:
                j = li - n_a
                (q,) = _proj(x, [wq[j]], "q_proj")
                if fresh:
                    o = _attn_prompt(q, k_sh, v_sh, sb_bias[j].astype(F32), n_seq)
                else:
                    nbias2 = sb_bias[j].astype(F32) * (-LOG2E)
                    bias_rows = jnp.broadcast_to(jnp.repeat(nbias2, t)[:, None], (n_heads * t, LANES))
                    o = _attn_sample(pad_rows(q), pad_rows(k_sh), pad_rows(v_sh), cache_k2, cache_v2,
                                     page_table, bias_rows, t)
                    o = o[:, :t].reshape(n, d)
                x = _oproj_postnorm(o, x, wo[j], vec(ln_g[li, 1]), vec(ln_b[li, 1]), alpha)
            x = _ffn_half(x, *ffn_w[li, 1], alpha)
            if li == n_a - 1:
                k_sh, v_sh = _proj(x, wkv, "kv_proj")
        return x, jnp.stack(new_wkv), jnp.stack(new_shift), k_sh, v_sh

    zero_state = jnp.zeros((n_a, bp, n_heads, HEAD_DIM, HEAD_DIM), F32)
    y_p, wkv_p, shift_p, k_p, v_p = trunk(x_prompt.reshape(bp * sp, d), bp, sp, zero_state, None)
    y_s, wkv_s, shift_s, k_s, v_s = trunk(x_sample.reshape(bs * ss, d), bs, ss, state_wkv, state_shift)

    def paged(a):
        return a.reshape(bp, sp // PAGE_SIZE, PAGE_SIZE, n_heads, HEAD_DIM)

    return (y_p.reshape(bp, sp, d), y_s.reshape(bs, ss, d), wkv_p, shift_p, paged(k_p), paged(v_p),
            wkv_s, shift_s, k_s.reshape(bs, ss, n_heads, HEAD_DIM), v_s.reshape(bs, ss, n_heads, HEAD_DIM))
```

```python
import functools

import jax
import jax.numpy as jnp
from jax import lax
from jax.experimental import pallas as pl
from jax.experimental.pallas import tpu as pltpu

HEAD_DIM = 64
LANES = 128
SUBLANES = 8
MXU_W = 256
LN_EPS = 1e-5
GN_EPS = 64e-5
SB_SCALE = HEAD_DIM ** -0.5
PAGE_SIZE = 128

FFN_CHUNK = MXU_W
TOKEN_TILE = 512
PRE_TILE = 256
SCAN_CHUNK = 64
SCAN_SEQS = 4
SCAN_SEQS_SAMPLE = 8
SAMPLE_PAD = SUBLANES
ATT_Q = 256
ATT_K = LANES
ATT_SUB = 16
PAGES_PER_STEP = 8
VMEM_LIMIT = 56 * 1024 * 1024

F32 = jnp.float32
BF16 = jnp.bfloat16
HI = lax.Precision.HIGHEST
NT = (((1,), (1,)), ((), ()))


def _const_spec(shape):
    nd = len(shape)
    return pl.BlockSpec(shape, lambda *_: (0,) * nd, pipeline_mode=pl.Buffered(1))


def _params(sem):
    return pltpu.CompilerParams(dimension_semantics=sem, vmem_limit_bytes=VMEM_LIMIT)


def _iota(shape, axis):
    return lax.broadcasted_iota(jnp.int32, shape, axis)


def _layer_norm(y, g, b):
    mu = jnp.mean(y, axis=-1, keepdims=True)
    yc = y - mu
    var = jnp.mean(yc * yc, axis=-1, keepdims=True)
    return yc * lax.rsqrt(var + LN_EPS) * g + b


def _sigmoid(x):
    return 1.0 / (1.0 + jnp.exp(-x))


def _split_bf16(x):
    hi = x.astype(BF16)
    lo = (x - hi.astype(F32)).astype(BF16)
    return hi, lo


def _mm(a, b):
    return jnp.dot(a.astype(BF16), b.astype(BF16), preferred_element_type=F32)


def _mm_nt(a, b):
    return lax.dot_general(a.astype(BF16), b.astype(BF16), NT, preferred_element_type=F32)


def _head_sum(q, sel, sel_t):
    hi, lo = _split_bf16(q)
    s = jnp.dot(hi, sel, preferred_element_type=F32) + jnp.dot(lo, sel, preferred_element_type=F32)
    hi, lo = _split_bf16(s)
    return jnp.dot(hi, sel_t, preferred_element_type=F32) + jnp.dot(lo, sel_t, preferred_element_type=F32)


def _ffn_kernel(x_ref, wg_ref, wu_ref, wd_ref, g_ref, b_ref, o_ref, acc_ref, *, alpha, n_chunks):
    x = x_ref[...]
    xb = x.astype(BF16)
    acc_ref[...] = jnp.zeros_like(acc_ref)

    def body(c, carry):
        gate = jnp.dot(xb, wg_ref[c], preferred_element_type=F32)
        up = jnp.dot(xb, wu_ref[c], preferred_element_type=F32)
        h = (gate * _sigmoid(gate) * up).astype(BF16)
        acc_ref[...] += jnp.dot(h, wd_ref[c], preferred_element_type=F32)
        return carry

    lax.fori_loop(0, n_chunks, body, 0, unroll=True)
    y = alpha * x + 0.5 * acc_ref[...]
    o_ref[...] = _layer_norm(y, g_ref[...], b_ref[...])


def _ffn_half(x, wg, wu, wd, g, b, alpha):
    n, d = x.shape
    n_chunks = wg.shape[0]
    tm = TOKEN_TILE
    kern = functools.partial(_ffn_kernel, alpha=alpha, n_chunks=n_chunks)
    return pl.pallas_call(
        kern,
        grid=(n // tm,),
        in_specs=[
            pl.BlockSpec((tm, d), lambda i: (i, 0)),
            _const_spec(wg.shape), _const_spec(wu.shape), _const_spec(wd.shape),
            _const_spec(g.shape), _const_spec(b.shape),
        ],
        out_specs=pl.BlockSpec((tm, d), lambda i: (i, 0)),
        out_shape=jax.ShapeDtypeStruct((n, d), F32),
        scratch_shapes=[pltpu.VMEM((tm, d), F32)],
        compiler_params=_params(("parallel",)),
        name="ffn_half",
    )(x, wg, wu, wd, g, b)


def _tm_pre_kernel(*refs, has_vres, seq_len, tile):
    x_ref, shift_ref = refs[0], refs[1]
    refs = refs[2:]
    if has_vres:
        vf_ref, refs = refs[0], refs[1:]
    (mu_ref, wr_ref, wk_ref, wv_ref, w0_ref, w1_ref, w2_ref, a0_ref, a1_ref, a2_ref) = refs[:10]
    refs = refs[10:]
    if has_vres:
        (v0_ref, v1_ref, v2_ref), refs = refs[:3], refs[3:]
    (g1_ref, g2_ref, kk_ref, ka_ref, sel_ref, selt_ref,
     r_out, lw_out, k_out, v_out, a_out, b_out, g_out) = refs
    x = x_ref[...]
    if seq_len is None:
        xs = shift_ref[...]
    else:
        starts_sequence = (pl.program_id(0) * tile) % seq_len == 0
        first = jnp.where(starts_sequence, 0.0, shift_ref[SUBLANES - 1:SUBLANES, :])
        xs = jnp.where(_iota(x.shape, 0) == 0, first, pltpu.roll(x, 1, 0))
    xx = xs - x

    def mix(i):
        return (x + xx * mu_ref[i:i + 1, :]).astype(BF16)

    def mm(a, w_ref):
        return jnp.dot(a, w_ref[...], preferred_element_type=F32)

    xr, xw, xk, xv, xa, xg = (mix(i) for i in range(6))
    r_out[...] = mm(xr, wr_ref)
    w_in = w0_ref[...] + mm(jnp.tanh(mm(xw, w1_ref)).astype(BF16), w2_ref)
    softplus = jnp.maximum(-w_in, 0.0) + jnp.log1p(jnp.exp(-jnp.abs(w_in)))
    lw_out[...] = -jnp.exp(-softplus - 0.5)
    k = mm(xk, wk_ref)
    v = mm(xv, wv_ref)
    if has_vres:
        gate = _sigmoid(v0_ref[...] + mm(mm(xv, v1_ref).astype(BF16), v2_ref))
        v = v + (vf_ref[...] - v) * gate
    v_out[...] = v
    a = _sigmoid(a0_ref[...] + mm(mm(xa, a1_ref).astype(BF16), a2_ref))
    g_out[...] = mm(_sigmoid(mm(xg, g1_ref)).astype(BF16), g2_ref)
    kk = k * kk_ref[...]
    norm = jnp.sqrt(_head_sum(kk * kk, sel_ref[...], selt_ref[...]))
    kk = kk / jnp.maximum(norm, 1e-12)
    k_out[...] = k * (1.0 + (a - 1.0) * ka_ref[...])
    a_out[...] = -kk
    b_out[...] = kk * a


def _tm_pre(x, x_shift, v_first, weights, seq_len):
    n, d = x.shape
    tm = PRE_TILE
    has_vres = v_first is not None
    tile = pl.BlockSpec((tm, d), lambda i: (i, 0))
    if x_shift is None:
        per = tm // SUBLANES
        shift_in = x
        shift_spec = pl.BlockSpec((SUBLANES, d), lambda i: (jnp.maximum(i * per - 1, 0), 0))
    else:
        seq_len = None
        shift_in = x_shift
        shift_spec = tile
    acts = [x, shift_in] + ([v_first] if has_vres else [])
    kern = functools.partial(_tm_pre_kernel, has_vres=has_vres, seq_len=seq_len, tile=tm)
    out = jax.ShapeDtypeStruct((n, d), F32)
    return pl.pallas_call(
        kern,
        grid=(n // tm,),
        in_specs=[tile, shift_spec] + [tile] * has_vres + [_const_spec(w.shape) for w in weights],
        out_specs=[tile] * 7,
        out_shape=[out] * 7,
        compiler_params=_params(("parallel",)),
        name="tm_pre_vres" if has_vres else "tm_pre",
    )(*acts, *weights)


def _scan_kernel(r_ref, lw_ref, k_ref, v_ref, a_ref, b_ref, s0_ref, y_ref, sout_ref, s_scr,
                 *, chunk, n_chunks, n_groups):
    c = pl.program_id(1)
    hd = HEAD_DIM
    gw = MXU_W
    hpg = gw // hd
    pw = hpg * chunk

    n_par = r_ref.shape[0]
    chains = [(s, g) for s in range(n_par) for g in range(n_groups)]

    @pl.when(c == 0)
    def _():
        s_scr[...] = jnp.zeros_like(s_scr)
        for i, (s, g) in enumerate(chains):
            for h in range(hpg):
                s_scr[i, h * hd:(h + 1) * hd, h * hd:(h + 1) * hd] = s0_ref[s, g * hpg + h]

    row_p = _iota((chunk, pw), 0)
    j_p = _iota((chunk, pw), 1) % chunk
    strict = j_p < row_p
    incl = j_p <= row_p
    tri = jnp.where(_iota((chunk, chunk), 1) <= _iota((chunk, chunk), 0), 1.0, 0.0).astype(BF16)
    bd_pp = (_iota((pw, pw), 0) // chunk) == (_iota((pw, pw), 1) // chunk)
    bd_pg = (_iota((pw, gw), 0) // chunk) == (_iota((pw, gw), 1) // hd)
    bd_gg = (_iota((gw, gw), 0) // hd) == (_iota((gw, gw), 1) // hd)

    def heads_on_rows(x, mask):
        return jnp.where(mask, jnp.concatenate([x] * hpg, axis=0), 0.0).astype(BF16)

    groups = range(len(chains))
    sls = [slice(g * gw, (g + 1) * gw) for _, g in chains]
    vs, kbs, w_lasts, ars, r_bs, r_ks = [], [], [], [], [], []
    for (s, _), sl in zip(chains, sls):
        r, lw, k, v, a, b = (ref[s, :, sl] for ref in (r_ref, lw_ref, k_ref, v_ref, a_ref, b_ref))
        hi = lw.astype(BF16)
        rem = lw - hi.astype(F32)
        mid = rem.astype(BF16)
        lo = (rem - mid.astype(F32)).astype(BF16)
        cum = (jnp.dot(tri, hi, preferred_element_type=F32) + jnp.dot(tri, mid, preferred_element_type=F32)
               + jnp.dot(tri, lo, preferred_element_type=F32))
        w_last = cum[chunk - 1:chunk, :]
        w_neg = jnp.exp(-cum)
        w_rest = jnp.exp(w_last - cum)
        vs.append(v)
        w_lasts.append(w_last)
        kbs.append(jnp.concatenate([k * w_rest, b * w_rest], axis=0))
        ars.append(jnp.concatenate([a * jnp.exp(cum - lw), r * jnp.exp(cum)], axis=0).astype(BF16))
        r_bs.append(heads_on_rows(b * w_neg, bd_pg))
        r_ks.append(heads_on_rows(k * w_neg, bd_pg))

    ss = [s_scr[g] for g in groups]
    g_bs = [_mm_nt(ars[g], r_bs[g]) for g in groups]
    g_ks = [_mm_nt(ars[g], r_ks[g]) for g in groups]
    from_state = [_mm_nt(ars[g], ss[g]) for g in groups]
    v_bds = [heads_on_rows(vs[g], bd_pg) for g in groups]
    a_abs = [jnp.where(strict, g_bs[g][:chunk], 0.0) for g in groups]
    a_rbs = [jnp.where(incl, g_bs[g][chunk:], 0.0) for g in groups]
    a_aks = [jnp.where(strict, g_ks[g][:chunk], 0.0) for g in groups]
    a_rks = [jnp.where(incl, g_ks[g][chunk:], 0.0) for g in groups]
    us = [from_state[g][:chunk] + _mm(a_aks[g], v_bds[g]) for g in groups]
    powers = a_abs
    span = 1
    while True:
        us = [us[g] + _mm(powers[g], heads_on_rows(us[g], bd_pg)) for g in groups]
        span *= 2
        if span >= chunk:
            break
        powers = [_mm(powers[g], heads_on_rows(powers[g], bd_pp)) for g in groups]
    for i, (s, _) in enumerate(chains):
        y_ref[s, :, sls[i]] = (from_state[i][chunk:] + _mm(a_rks[i], v_bds[i])
                               + _mm(a_rbs[i], heads_on_rows(us[i], bd_pg)))
    for g in groups:
        vu_t = jnp.transpose(jnp.concatenate([vs[g], us[g]], axis=0))
        s_scr[g] = ss[g] * jnp.exp(w_lasts[g]) + jnp.where(bd_gg, _mm(vu_t, kbs[g]), 0.0)

    @pl.when(c == n_chunks - 1)
    def _():
        for i, (s, g) in enumerate(chains):
            for h in range(hpg):
                sout_ref[s, g * hpg + h] = s_scr[i, h * hd:(h + 1) * hd, h * hd:(h + 1) * hd]


def _rwkv_scan(r, lw, k, v, a, b, s0, chunk, n_par):
    n, d = r.shape
    n_seq, n_heads = s0.shape[0], s0.shape[1]
    t = n // n_seq
    n_chunks = t // chunk
    n_groups = d // MXU_W
    n_par = min(n_par, n_seq)
    assert n_seq % n_par == 0
    tile = pl.BlockSpec((n_par, chunk, d), lambda bi, c: (bi, c, 0))
    st = pl.BlockSpec((n_par, n_heads, HEAD_DIM, HEAD_DIM), lambda bi, c: (bi, 0, 0, 0))
    kern = functools.partial(_scan_kernel, chunk=chunk, n_chunks=n_chunks, n_groups=n_groups)
    y, s_new = pl.pallas_call(
        kern,
        grid=(n_seq // n_par, n_chunks),
        in_specs=[tile] * 6 + [st],
        out_specs=[tile, st],
        out_shape=[jax.ShapeDtypeStruct((n_seq, t, d), F32), jax.ShapeDtypeStruct(s0.shape, F32)],
        scratch_shapes=[pltpu.VMEM((n_par * n_groups, MXU_W, MXU_W), F32)],
        compiler_params=_params(("parallel", "arbitrary")),
        name=f"rwkv_scan_c{chunk}",
    )(*(z.reshape(n_seq, t, d) for z in (r, lw, k, v, a, b)), s0)
    return y.reshape(n, d), s_new


def _tm_post_kernel(y_ref, r_ref, k_ref, v_ref, g_ref, x_ref, wo_ref, gng_ref, gnb_ref, rk_ref,
                    lng_ref, lnb_ref, sel_ref, selt_ref, o_ref, *, alpha):
    sel, selt = sel_ref[...], selt_ref[...]
    y = y_ref[...]
    inv = 1.0 / HEAD_DIM
    yc = y - _head_sum(y, sel, selt) * inv
    var = _head_sum(yc * yc, sel, selt) * inv
    yn = yc * lax.rsqrt(var + GN_EPS) * gng_ref[...] + gnb_ref[...]
    bonus = _head_sum(r_ref[...] * k_ref[...] * rk_ref[...], sel, selt) * v_ref[...]
    z = ((yn + bonus) * g_ref[...]).astype(BF16)
    out = jnp.dot(z, wo_ref[...], preferred_element_type=F32)
    o_ref[...] = _layer_norm(alpha * x_ref[...] + out, lng_ref[...], lnb_ref[...])


def _tm_post(y, r, k, v, g, x, weights, alpha):
    n, d = x.shape
    tm = PRE_TILE
    tile = pl.BlockSpec((tm, d), lambda i: (i, 0))
    return pl.pallas_call(
        functools.partial(_tm_post_kernel, alpha=alpha),
        grid=(n // tm,),
        in_specs=[tile] * 6 + [_const_spec(w.shape) for w in weights],
        out_specs=tile,
        out_shape=jax.ShapeDtypeStruct((n, d), F32),
        compiler_params=_params(("parallel",)),
        name="tm_post",
    )(y, r, k, v, g, x, *weights)


def _proj_kernel(x_ref, *refs):
    n_out = len(refs) // 2
    xb = x_ref[...].astype(BF16)
    for w_ref, o_ref in zip(refs[:n_out], refs[n_out:]):
        o_ref[...] = jnp.dot(xb, w_ref[...], preferred_element_type=F32)


def _proj(x, ws, name):
    n, d = x.shape
    tm = TOKEN_TILE
    tile = pl.BlockSpec((tm, d), lambda i: (i, 0))
    return pl.pallas_call(
        _proj_kernel,
        grid=(n // tm,),
        in_specs=[tile] + [_const_spec(w.shape) for w in ws],
        out_specs=[pl.BlockSpec((tm, w.shape[1]), lambda i: (i, 0)) for w in ws],
        out_shape=[jax.ShapeDtypeStruct((n, w.shape[1]), F32) for w in ws],
        compiler_params=_params(("parallel",)),
        name=name,
    )(x, *ws)


def _oproj_kernel(o_ref, x_ref, w_ref, g_ref, b_ref, out_ref, *, alpha):
    sub = jnp.dot(o_ref[...].astype(BF16), w_ref[...], preferred_element_type=F32)
    out_ref[...] = _layer_norm(alpha * x_ref[...] + sub, g_ref[...], b_ref[...])


def _oproj_postnorm(o, x, w, g, b, alpha):
    n, d = x.shape
    tm = TOKEN_TILE
    tile = pl.BlockSpec((tm, d), lambda i: (i, 0))
    return pl.pallas_call(
        functools.partial(_oproj_kernel, alpha=alpha),
        grid=(n // tm,),
        in_specs=[tile, tile, _const_spec(w.shape), _const_spec(g.shape), _const_spec(b.shape)],
        out_specs=tile,
        out_shape=jax.ShapeDtypeStruct((n, d), F32),
        compiler_params=_params(("parallel",)),
        name="oproj_postnorm",
    )(o, x, w, g, b)


LOG2E = 1.4426950408889634


def _log2_survive(nz2):
    sign = jnp.uint32(0x80000000)
    neg_abs = lax.bitcast_convert_type(lax.bitcast_convert_type(nz2, jnp.uint32) | sign, F32)
    return jnp.minimum(nz2, 0.0) - jnp.log2(1.0 + jnp.exp2(neg_abs))


def _attn_prompt_kernel(bias_ref, q_ref, k_ref, v_ref, o_ref, *, tq, tk, n_sub):
    p = pl.program_id(1)
    qb = pl.program_id(2)
    hd = HEAD_DIM
    q = (q_ref[...] * (-SB_SCALE * LOG2E)).astype(BF16)
    bias = jnp.where(_iota((1, 2 * tk), 1) < tk, bias_ref[2 * p], bias_ref[2 * p + 1]) * (-LOG2E)
    row = _iota((tq, tk), 0)
    col = _iota((tq, tk), 1)
    head0_k = _iota((tk, LANES), 1) < hd
    r2 = _iota((tk, 2 * tk), 0)
    c2 = _iota((tk, 2 * tk), 1)
    suffix = jnp.where((c2 >= tk) | (r2 > c2), 1.0, 0.0).astype(BF16)

    kpi = n_sub * tk
    per_q = tq // tk
    n_phase = kpi // tq
    diag = (qb * tq) // kpi

    def heads_on_rows(blk):
        return jnp.concatenate([jnp.where(head0_k, blk, 0.0), jnp.where(head0_k, 0.0, blk)], axis=0).astype(BF16)

    def key_blocks(start, carry, phase):
        acc, car0, car1 = carry
        n_act = n_sub if phase is None else (phase + 1) * per_q
        subs = list(reversed(range(n_act)))
        st = [dict() for _ in subs]

        def stage(i, s):
            sub, c = subs[i], st[i]
            masked = phase is not None and sub >= phase * per_q
            if s == 0:
                kbd = heads_on_rows(k_ref[pl.ds(start + sub * tk, tk), :])
                c["nz"] = lax.dot_general(q, kbd, NT, preferred_element_type=F32) + bias
            elif s == 1:
                l1m = _log2_survive(c["nz"])
                if masked:
                    c["valid"] = jnp.concatenate([(col + (sub - phase * per_q) * tk) < row] * 2, axis=1)
                    l1m = jnp.where(c["valid"], l1m, 0.0)
                c["l1m"] = l1m
            elif s == 2:
                l1m = c["l1m"].astype(BF16)
                c["cs"] = [jnp.dot(l1m[:, h * tk:(h + 1) * tk], suffix, preferred_element_type=F32)
                           for h in range(2)]
            elif s == 3:
                nonlocal car0, car1
                surv = jnp.concatenate([c["cs"][0][:, :tk] + car0, c["cs"][1][:, :tk] + car1], axis=1)
                w = jnp.exp2(c["l1m"] - c["nz"] + surv)
                if masked:
                    w = jnp.where(c["valid"], w, 0.0)
                c["w"] = w.astype(BF16)
                car0 = car0 + c["cs"][0][:, tk:]
                car1 = car1 + c["cs"][1][:, tk:]
            else:
                nonlocal acc
                vbd = heads_on_rows(v_ref[pl.ds(start + sub * tk, tk), :])
                acc = acc + jnp.dot(c["w"], vbd, preferred_element_type=F32)

        n_stages = 5
        for t in range(n_act + n_stages - 1):
            for i in range(n_act):
                if 0 <= t - i < n_stages:
                    stage(i, t - i)
        return acc, car0, car1

    zero = jnp.zeros((tq, LANES), F32)
    if kpi == k_ref.shape[0]:
        carry = lax.switch(qb, [functools.partial(key_blocks, 0, (zero, zero, zero), ph)
                                for ph in range(n_phase)])
    else:
        diag_start = pl.multiple_of(diag * kpi, kpi)
        carry = lax.switch(qb % n_phase,
                           [functools.partial(key_blocks, diag_start, (zero, zero, zero), ph)
                            for ph in range(n_phase)])

        def body(i, carry):
            return key_blocks(pl.multiple_of((diag - i) * kpi, kpi), carry, None)

        carry = lax.fori_loop(1, diag + 1, body, carry)
    o_ref[...] = carry[0]


def _attn_prompt(q, k, v, bias, n_seq):
    n, d = q.shape
    t = n // n_seq
    tq, tk = ATT_Q, ATT_K
    n_sub = min(ATT_SUB, t // tk)
    assert tk == LANES and tq % tk == 0 and (n_sub * tk) % tq == 0 and t % (n_sub * tk) == 0
    n_blk = t // tq
    n_pairs = d // LANES
    return pl.pallas_call(
        functools.partial(_attn_prompt_kernel, tq=tq, tk=tk, n_sub=n_sub),
        grid_spec=pltpu.PrefetchScalarGridSpec(
            num_scalar_prefetch=1,
            grid=(n_seq, n_pairs, n_blk),
            in_specs=[
                pl.BlockSpec((tq, LANES), lambda bi, p, qb, bias_ref: (bi * n_blk + qb, p)),
                pl.BlockSpec((t, LANES), lambda bi, p, qb, bias_ref: (bi, p)),
                pl.BlockSpec((t, LANES), lambda bi, p, qb, bias_ref: (bi, p)),
            ],
            out_specs=pl.BlockSpec((tq, LANES), lambda bi, p, qb, bias_ref: (bi * n_blk + qb, p)),
        ),
        out_shape=jax.ShapeDtypeStruct((n, d), F32),
        compiler_params=_params(("parallel", "parallel", "arbitrary")),
        name="attn_prompt",
    )(bias, q, k, v)


def _attn_sample_kernel(pt_ref, q_ref, kn_ref, vn_ref, *refs, n_new, n_steps, n_pg):
    kc_refs, vc_refs = refs[:n_pg], refs[n_pg:2 * n_pg]
    biasc_ref, o_ref, qrep_scr, kpad_scr, vpad_scr, acc_scr, carry_scr = refs[2 * n_pg:]
    j = pl.program_id(1)
    page = LANES
    d = q_ref.shape[2]
    n_rows = (d // HEAD_DIM) * n_new
    own_head = (_iota((n_rows, d), 0) // n_new) == (_iota((n_rows, d), 1) // HEAD_DIM)
    r2 = _iota((page, 2 * page), 0)
    c2 = _iota((page, 2 * page), 1)
    suffix = jnp.where((c2 >= page) | (r2 > c2), 1.0, 0.0).astype(BF16)

    def visit(scores, values, n_blk, valid):
        nz = scores + jnp.concatenate([biasc_ref[...]] * n_blk, axis=1)
        l1m = _log2_survive(nz)
        if valid is not None:
            l1m = jnp.where(valid, l1m, 0.0)
        blks = [slice(i * page, (i + 1) * page) for i in range(n_blk)]
        cs = [jnp.dot(l1m[:, sl].astype(BF16), suffix, preferred_element_type=F32) for sl in blks]
        carry = carry_scr[...]
        ws = [None] * n_blk
        for i in reversed(range(n_blk)):
            ws[i] = jnp.exp2(l1m[:, blks[i]] - nz[:, blks[i]] + cs[i][:, :page] + carry)
            carry = carry + cs[i][:, page:]
        carry_scr[...] = carry
        if valid is not None:
            ws = [jnp.where(valid, ws[0], 0.0)]
        acc = acc_scr[...]
        for i in range(n_blk):
            acc = acc + values(i, ws[i].astype(BF16))
        acc_scr[...] = acc

    @pl.when(j == 0)
    def _():
        rr = _iota((n_rows, SAMPLE_PAD), 0)
        cc = _iota((n_rows, SAMPLE_PAD), 1)
        rep = jnp.where((rr % n_new) == cc, 1.0, 0.0).astype(BF16)
        qrep = jnp.dot(rep, (q_ref[0] * (-SB_SCALE * LOG2E)).astype(BF16), preferred_element_type=F32)
        qrep_scr[...] = jnp.where(own_head, qrep, 0.0).astype(BF16)
        kpad_scr[...] = jnp.zeros_like(kpad_scr)
        vpad_scr[...] = jnp.zeros_like(vpad_scr)
        kpad_scr[0:SAMPLE_PAD, :] = kn_ref[0].astype(BF16)
        vpad_scr[0:SAMPLE_PAD, :] = vn_ref[0].astype(BF16)
        acc_scr[...] = jnp.zeros_like(acc_scr)
        carry_scr[...] = jnp.zeros_like(carry_scr)
        valid = _iota((n_rows, page), 1) < (_iota((n_rows, page), 0) % n_new)
        scores = lax.dot_general(qrep_scr[...], kpad_scr[...], NT, preferred_element_type=F32)
        visit(scores, lambda i, w: jnp.dot(w, vpad_scr[...], preferred_element_type=F32), 1, valid)

    qrep = qrep_scr[...]
    scores = jnp.concatenate([jnp.dot(qrep, kc_refs[i][0].astype(BF16), preferred_element_type=F32)
                              for i in range(n_pg)], axis=1)

    def values(i, w):
        return lax.dot_general(w, vc_refs[i][0].astype(BF16), NT, preferred_element_type=F32)

    visit(scores, values, n_pg, None)

    @pl.when(j == n_steps - 1)
    def _():
        rr = _iota((SAMPLE_PAD, n_rows), 0)
        cc = _iota((SAMPLE_PAD, n_rows), 1)
        pick = jnp.where((cc % n_new) == rr, 1.0, 0.0)
        own = jnp.where(own_head, acc_scr[...], 0.0)
        o_ref[0] = jnp.dot(pick, own, preferred_element_type=F32, precision=HI)


def _attn_sample(q, k_new, v_new, cache_k, cache_v, page_table, bias_rows, n_new):
    n_seq, pad, d = q.shape
    n_pages = page_table.shape[1]
    page = cache_k.shape[2]
    assert page == LANES
    n_pg = min(PAGES_PER_STEP, n_pages)
    assert n_pages % n_pg == 0
    n_steps = n_pages // n_pg

    def page_spec(i):
        def index(bi, j, pt_ref):
            return (pt_ref[bi, n_pages - n_pg * (j + 1) + i], 0, 0)
        return pl.BlockSpec((1, d, page), index)

    row = pl.BlockSpec((1, pad, d), lambda bi, j, pt_ref: (bi, 0, 0))
    pages = [page_spec(i) for i in range(n_pg)]
    n_rows = bias_rows.shape[0]
    return pl.pallas_call(
        functools.partial(_attn_sample_kernel, n_new=n_new, n_steps=n_steps, n_pg=n_pg),
        grid_spec=pltpu.PrefetchScalarGridSpec(
            num_scalar_prefetch=1,
            grid=(n_seq, n_steps),
            in_specs=[row, row, row] + pages + pages
                     + [pl.BlockSpec((n_rows, LANES), lambda bi, j, pt_ref: (0, 0))],
            out_specs=row,
            scratch_shapes=[
                pltpu.VMEM((n_rows, d), BF16),
                pltpu.VMEM((page, d), BF16),
                pltpu.VMEM((page, d), BF16),
                pltpu.VMEM((n_rows, d), F32),
                pltpu.VMEM((n_rows, LANES), F32),
            ],
        ),
        out_shape=jax.ShapeDtypeStruct((n_seq, pad, d), F32),
        compiler_params=_params(("parallel", "arbitrary")),
        name="attn_sample",
    )(page_table, q, k_new, v_new, *([cache_k] * n_pg), *([cache_v] * n_pg), bias_rows)


def kernel(x_prompt, x_sample, cache_k, cache_v, state_wkv, state_shift, page_table, ln_g, ln_b, ffn_w_gate, ffn_w_up, ffn_w_down, tm_mu, tm_w0, tm_w1, tm_w2, tm_a0, tm_a1, tm_a2, tm_v0, tm_v1, tm_v2, tm_g1, tm_g2, tm_k_k, tm_k_a, tm_r_k, tm_w_r, tm_w_k, tm_w_v, tm_w_o, tm_gn_g, tm_gn_b, sb_w_k, sb_w_v, sb_w_q, sb_bias, sb_w_o):
    bp, sp, d = x_prompt.shape
    bs, ss, _ = x_sample.shape
    depth = ln_g.shape[0]
    n_a = tm_mu.shape[0]
    n_heads = d // HEAD_DIM
    n_ff = ffn_w_gate.shape[-1] // FFN_CHUNK
    alpha = (2 * depth) ** 0.25

    def vec(a):
        return a.reshape(1, -1).astype(F32)

    def bf(a):
        return a.astype(BF16)

    hid = jnp.arange(d) // HEAD_DIM
    sel = (hid[:, None] == jnp.arange(LANES)[None, :]).astype(BF16)
    sel_t = sel.T

    ffn_w = {}
    for li in range(depth):
        for j in range(2):
            ffn_w[li, j] = (
                bf(ffn_w_gate[li, j]).reshape(d, n_ff, FFN_CHUNK).transpose(1, 0, 2),
                bf(ffn_w_up[li, j]).reshape(d, n_ff, FFN_CHUNK).transpose(1, 0, 2),
                bf(ffn_w_down[li, j]).reshape(n_ff, FFN_CHUNK, d),
                vec(ln_g[li, 2 * j]), vec(ln_b[li, 2 * j]))
    pre_w, post_w = [], []
    for li in range(n_a):
        w = [tm_mu[li], bf(tm_w_r[li]), bf(tm_w_k[li]), bf(tm_w_v[li]),
             vec(tm_w0[li]), bf(tm_w1[li]), bf(tm_w2[li]), vec(tm_a0[li]), bf(tm_a1[li]), bf(tm_a2[li])]
        if li > 0:
            w += [vec(tm_v0[li - 1]), bf(tm_v1[li - 1]), bf(tm_v2[li - 1])]
        w += [bf(tm_g1[li]), bf(tm_g2[li]), vec(tm_k_k[li]), vec(tm_k_a[li]), sel, sel_t]
        pre_w.append(w)
        post_w.append([bf(tm_w_o[li]), vec(tm_gn_g[li]), vec(tm_gn_b[li]), vec(tm_r_k[li]),
                       vec(ln_g[li, 1]), vec(ln_b[li, 1]), sel, sel_t])
    wq = [bf(sb_w_q[j]) for j in range(depth - n_a)]
    wo = [bf(sb_w_o[j]) for j in range(depth - n_a)]
    wkv = [bf(sb_w_k), bf(sb_w_v)]
    n_phys, page = cache_k.shape[0], cache_k.shape[1]
    cache_k2 = jnp.transpose(cache_k, (0, 2, 3, 1)).reshape(n_phys, d, page)
    cache_v2 = jnp.transpose(cache_v, (0, 2, 3, 1)).reshape(n_phys, d, page)

    def trunk(x, n_seq, t, wkv_in, shift_in):
        fresh = shift_in is None
        n = n_seq * t
        new_wkv, new_shift = [], []
        v_first = None
        k_sh = v_sh = k_new = v_new = None

        def pad_rows(a):
            a = a.reshape(n_seq, t, d)
            return jnp.pad(a, ((0, 0), (0, SAMPLE_PAD - t), (0, 0)))

        for li in range(depth):
            x = _ffn_half(x, *ffn_w[li, 0], alpha)
            if li < n_a:
                x3 = x.reshape(n_seq, t, d)
                new_shift.append(x3[:, -1])
                if fresh:
                    x_shift = None
                else:
                    x_shift = jnp.concatenate([shift_in[li][:, None, :], x3[:, :-1]], axis=1).reshape(n, d)
                r, lw, k, v, a, b, g = _tm_pre(x, x_shift, v_first if li > 0 else None, pre_w[li], t)
                if li == 0:
                    v_first = v
                if fresh:
                    y, s_new = _rwkv_scan(r, lw, k, v, a, b, wkv_in[li], SCAN_CHUNK, SCAN_SEQS)
                else:
                    padded = (pad_rows(z).reshape(n_seq * SAMPLE_PAD, d) for z in (r, lw, k, v, a, b))
                    y, s_new = _rwkv_scan(*padded, wkv_in[li], SAMPLE_PAD, SCAN_SEQS_SAMPLE)
                    y = y.reshape(n_seq, SAMPLE_PAD, d)[:, :t].reshape(n, d)
                new_wkv.append(s_new)
                x = _tm_post(y, r, k, v, g, x, post_w[li], alpha)
            else:
                j = li - n_a
                (q,) = _proj(x, [wq[j]], "q_proj")
                if fresh:
                    o = _attn_prompt(q, k_sh, v_sh, sb_bias[j].astype(F32), n_seq)
                else:
                    nbias2 = sb_bias[j].astype(F32) * (-LOG2E)
                    bias_rows = jnp.broadcast_to(jnp.repeat(nbias2, t)[:, None], (n_heads * t, LANES))
                    o = _attn_sample(pad_rows(q), k_new, v_new, cache_k2, cache_v2, page_table, bias_rows, t)
                    o = o[:, :t].reshape(n, d)
                x = _oproj_postnorm(o, x, wo[j], vec(ln_g[li, 1]), vec(ln_b[li, 1]), alpha)
            x = _ffn_half(x, *ffn_w[li, 1], alpha)
            if li == n_a - 1:
                k_sh, v_sh = _proj(x, wkv, "kv_proj")
                if not fresh:
                    k_new, v_new = pad_rows(k_sh), pad_rows(v_sh)
        return x, jnp.stack(new_wkv), jnp.stack(new_shift), k_sh, v_sh

    zero_state = jnp.zeros((n_a, bp, n_heads, HEAD_DIM, HEAD_DIM), F32)
    y_p, wkv_p, shift_p, k_p, v_p = trunk(x_prompt.reshape(bp * sp, d), bp, sp, zero_state, None)
    y_s, wkv_s, shift_s, k_s, v_s = trunk(x_sample.reshape(bs * ss, d), bs, ss, state_wkv, state_shift)

    def paged(a):
        return a.reshape(bp, sp // PAGE_SIZE, PAGE_SIZE, n_heads, HEAD_DIM)

    return (y_p.reshape(bp, sp, d), y_s.reshape(bs, ss, d), wkv_p, shift_p, paged(k_p), paged(v_p),
            wkv_s, shift_s, k_s.reshape(bs, ss, n_heads, HEAD_DIM), v_s.reshape(bs, ss, n_heads, HEAD_DIM))
```

```python
import functools

import jax
import jax.numpy as jnp
from jax import lax
from jax.experimental import pallas as pl
from jax.experimental.pallas import tpu as pltpu

HEAD_DIM = 64
LANES = 128
SUBLANES = 8
MXU_W = 256
LN_EPS = 1e-5
GN_EPS = 64e-5
SB_SCALE = HEAD_DIM ** -0.5
PAGE_SIZE = 128

FFN_CHUNK = MXU_W
TOKEN_TILE = 512
PRE_TILE = 256
SCAN_CHUNK = 64
SCAN_SEQS = 4
SCAN_SEQS_SAMPLE = 8
SAMPLE_PAD = SUBLANES
ATT_Q = 256
ATT_K = LANES
ATT_SUB = 16
PAGES_PER_STEP = 16
VMEM_LIMIT = 56 * 1024 * 1024

F32 = jnp.float32
BF16 = jnp.bfloat16
HI = lax.Precision.HIGHEST
NT = (((1,), (1,)), ((), ()))


def _const_spec(shape):
    nd = len(shape)
    return pl.BlockSpec(shape, lambda *_: (0,) * nd, pipeline_mode=pl.Buffered(1))


def _params(sem):
    return pltpu.CompilerParams(dimension_semantics=sem, vmem_limit_bytes=VMEM_LIMIT)


def _iota(shape, axis):
    return lax.broadcasted_iota(jnp.int32, shape, axis)


def _layer_norm(y, g, b):
    mu = jnp.mean(y, axis=-1, keepdims=True)
    yc = y - mu
    var = jnp.mean(yc * yc, axis=-1, keepdims=True)
    return yc * lax.rsqrt(var + LN_EPS) * g + b


def _sigmoid(x):
    return 1.0 / (1.0 + jnp.exp(-x))


def _split_bf16(x):
    hi = x.astype(BF16)
    lo = (x - hi.astype(F32)).astype(BF16)
    return hi, lo


def _mm(a, b):
    return jnp.dot(a.astype(BF16), b.astype(BF16), preferred_element_type=F32)


def _mm_nt(a, b):
    return lax.dot_general(a.astype(BF16), b.astype(BF16), NT, preferred_element_type=F32)


def _head_sum(q, sel, sel_t):
    hi, lo = _split_bf16(q)
    s = jnp.dot(hi, sel, preferred_element_type=F32) + jnp.dot(lo, sel, preferred_element_type=F32)
    hi, lo = _split_bf16(s)
    return jnp.dot(hi, sel_t, preferred_element_type=F32) + jnp.dot(lo, sel_t, preferred_element_type=F32)


def _ffn_kernel(x_ref, wg_ref, wu_ref, wd_ref, g_ref, b_ref, o_ref, acc_ref, *, alpha, n_chunks):
    x = x_ref[...]
    xb = x.astype(BF16)
    acc_ref[...] = jnp.zeros_like(acc_ref)

    for c in range(n_chunks):
        cols = slice(c * FFN_CHUNK, (c + 1) * FFN_CHUNK)
        gate = jnp.dot(xb, wg_ref[:, cols].astype(BF16), preferred_element_type=F32)
        up = jnp.dot(xb, wu_ref[:, cols].astype(BF16), preferred_element_type=F32)
        h = (gate * _sigmoid(gate) * up).astype(BF16)
        acc_ref[...] += jnp.dot(h, wd_ref[cols, :].astype(BF16), preferred_element_type=F32)
    y = alpha * x + 0.5 * acc_ref[...]
    o_ref[...] = _layer_norm(y, g_ref[...], b_ref[...])


def _ffn_half(x, wg, wu, wd, g, b, alpha):
    n, d = x.shape
    n_chunks = wg.shape[1] // FFN_CHUNK
    tm = TOKEN_TILE
    kern = functools.partial(_ffn_kernel, alpha=alpha, n_chunks=n_chunks)
    return pl.pallas_call(
        kern,
        grid=(n // tm,),
        in_specs=[
            pl.BlockSpec((tm, d), lambda i: (i, 0)),
            _const_spec(wg.shape), _const_spec(wu.shape), _const_spec(wd.shape),
            _const_spec(g.shape), _const_spec(b.shape),
        ],
        out_specs=pl.BlockSpec((tm, d), lambda i: (i, 0)),
        out_shape=jax.ShapeDtypeStruct((n, d), F32),
        scratch_shapes=[pltpu.VMEM((tm, d), F32)],
        compiler_params=_params(("parallel",)),
        name="ffn_half",
    )(x, wg, wu, wd, g, b)


def _tm_pre_kernel(*refs, has_vres, seq_len, tile):
    x_ref, shift_ref = refs[0], refs[1]
    refs = refs[2:]
    if has_vres:
        vf_ref, refs = refs[0], refs[1:]
    (mu_ref, wr_ref, wk_ref, wv_ref, w0_ref, w1_ref, w2_ref, a0_ref, a1_ref, a2_ref) = refs[:10]
    refs = refs[10:]
    if has_vres:
        (v0_ref, v1_ref, v2_ref), refs = refs[:3], refs[3:]
    (g1_ref, g2_ref, kk_ref, ka_ref, sel_ref, selt_ref,
     r_out, lw_out, k_out, v_out, a_out, b_out, g_out) = refs
    x = x_ref[...]
    if seq_len is None:
        xs = shift_ref[...]
    else:
        starts_sequence = (pl.program_id(0) * tile) % seq_len == 0
        first = jnp.where(starts_sequence, 0.0, shift_ref[SUBLANES - 1:SUBLANES, :])
        xs = jnp.where(_iota(x.shape, 0) == 0, first, pltpu.roll(x, 1, 0))
    xx = xs - x

    def mix(i):
        return (x + xx * mu_ref[i:i + 1, :]).astype(BF16)

    def mm(a, w_ref):
        return jnp.dot(a, w_ref[...], preferred_element_type=F32)

    xr, xw, xk, xv, xa, xg = (mix(i) for i in range(6))
    r_out[...] = mm(xr, wr_ref)
    w_in = w0_ref[...] + mm(jnp.tanh(mm(xw, w1_ref)).astype(BF16), w2_ref)
    softplus = jnp.maximum(-w_in, 0.0) + jnp.log1p(jnp.exp(-jnp.abs(w_in)))
    lw_out[...] = -jnp.exp(-softplus - 0.5)
    k = mm(xk, wk_ref)
    v = mm(xv, wv_ref)
    if has_vres:
        gate = _sigmoid(v0_ref[...] + mm(mm(xv, v1_ref).astype(BF16), v2_ref))
        v = v + (vf_ref[...] - v) * gate
    v_out[...] = v
    a = _sigmoid(a0_ref[...] + mm(mm(xa, a1_ref).astype(BF16), a2_ref))
    g_out[...] = mm(_sigmoid(mm(xg, g1_ref)).astype(BF16), g2_ref)
    kk = k * kk_ref[...]
    norm = jnp.sqrt(_head_sum(kk * kk, sel_ref[...], selt_ref[...]))
    kk = kk / jnp.maximum(norm, 1e-12)
    k_out[...] = k * (1.0 + (a - 1.0) * ka_ref[...])
    a_out[...] = -kk
    b_out[...] = kk * a


def _tm_pre(x, x_shift, v_first, weights, seq_len):
    n, d = x.shape
    tm = PRE_TILE
    has_vres = v_first is not None
    tile = pl.BlockSpec((tm, d), lambda i: (i, 0))
    if x_shift is None:
        per = tm // SUBLANES
        shift_in = x
        shift_spec = pl.BlockSpec((SUBLANES, d), lambda i: (jnp.maximum(i * per - 1, 0), 0))
    else:
        seq_len = None
        shift_in = x_shift
        shift_spec = tile
    acts = [x, shift_in] + ([v_first] if has_vres else [])
    kern = functools.partial(_tm_pre_kernel, has_vres=has_vres, seq_len=seq_len, tile=tm)
    out = jax.ShapeDtypeStruct((n, d), F32)
    return pl.pallas_call(
        kern,
        grid=(n // tm,),
        in_specs=[tile, shift_spec] + [tile] * has_vres + [_const_spec(w.shape) for w in weights],
        out_specs=[tile] * 7,
        out_shape=[out] * 7,
        compiler_params=_params(("parallel",)),
        name="tm_pre_vres" if has_vres else "tm_pre",
    )(*acts, *weights)


def _scan_kernel(r_ref, lw_ref, k_ref, v_ref, a_ref, b_ref, s0_ref, y_ref, sout_ref, s_scr,
                 *, chunk, n_chunks, n_groups):
    c = pl.program_id(1)
    hd = HEAD_DIM
    gw = MXU_W
    hpg = gw // hd
    pw = hpg * chunk

    n_par = r_ref.shape[0]
    chains = [(s, g) for s in range(n_par) for g in range(n_groups)]

    @pl.when(c == 0)
    def _():
        s_scr[...] = jnp.zeros_like(s_scr)
        for i, (s, g) in enumerate(chains):
            for h in range(hpg):
                s_scr[i, h * hd:(h + 1) * hd, h * hd:(h + 1) * hd] = s0_ref[s, g * hpg + h]

    row_p = _iota((chunk, pw), 0)
    j_p = _iota((chunk, pw), 1) % chunk
    strict = j_p < row_p
    incl = j_p <= row_p
    tri = jnp.where(_iota((chunk, chunk), 1) <= _iota((chunk, chunk), 0), 1.0, 0.0).astype(BF16)
    bd_pp = (_iota((pw, pw), 0) // chunk) == (_iota((pw, pw), 1) // chunk)
    bd_pg = (_iota((pw, gw), 0) // chunk) == (_iota((pw, gw), 1) // hd)
    bd_gg = (_iota((gw, gw), 0) // hd) == (_iota((gw, gw), 1) // hd)

    def heads_on_rows(x, mask):
        return jnp.where(mask, jnp.concatenate([x] * hpg, axis=0), 0.0).astype(BF16)

    groups = range(len(chains))
    sls = [slice(g * gw, (g + 1) * gw) for _, g in chains]
    vs, kbs, w_lasts, ars, r_bs, r_ks = [], [], [], [], [], []
    for (s, _), sl in zip(chains, sls):
        r, lw, k, v, a, b = (ref[s, :, sl] for ref in (r_ref, lw_ref, k_ref, v_ref, a_ref, b_ref))
        hi = lw.astype(BF16)
        rem = lw - hi.astype(F32)
        mid = rem.astype(BF16)
        lo = (rem - mid.astype(F32)).astype(BF16)
        cum = (jnp.dot(tri, hi, preferred_element_type=F32) + jnp.dot(tri, mid, preferred_element_type=F32)
               + jnp.dot(tri, lo, preferred_element_type=F32))
        w_last = cum[chunk - 1:chunk, :]
        w_neg = jnp.exp(-cum)
        w_rest = jnp.exp(w_last - cum)
        vs.append(v)
        w_lasts.append(w_last)
        kbs.append(jnp.concatenate([k * w_rest, b * w_rest], axis=0))
        ars.append(jnp.concatenate([a * jnp.exp(cum - lw), r * jnp.exp(cum)], axis=0).astype(BF16))
        r_bs.append(heads_on_rows(b * w_neg, bd_pg))
        r_ks.append(heads_on_rows(k * w_neg, bd_pg))

    ss = [s_scr[g] for g in groups]
    g_bs = [_mm_nt(ars[g], r_bs[g]) for g in groups]
    g_ks = [_mm_nt(ars[g], r_ks[g]) for g in groups]
    from_state = [_mm_nt(ars[g], ss[g]) for g in groups]
    v_bds = [heads_on_rows(vs[g], bd_pg) for g in groups]
    a_abs = [jnp.where(strict, g_bs[g][:chunk], 0.0) for g in groups]
    a_rbs = [jnp.where(incl, g_bs[g][chunk:], 0.0) for g in groups]
    a_aks = [jnp.where(strict, g_ks[g][:chunk], 0.0) for g in groups]
    a_rks = [jnp.where(incl, g_ks[g][chunk:], 0.0) for g in groups]
    us = [from_state[g][:chunk] + _mm(a_aks[g], v_bds[g]) for g in groups]
    powers = a_abs
    span = 1
    while True:
        us = [us[g] + _mm(powers[g], heads_on_rows(us[g], bd_pg)) for g in groups]
        span *= 2
        if span >= chunk:
            break
        powers = [_mm(powers[g], heads_on_rows(powers[g], bd_pp)) for g in groups]
    for i, (s, _) in enumerate(chains):
        y_ref[s, :, sls[i]] = (from_state[i][chunk:] + _mm(a_rks[i], v_bds[i])
                               + _mm(a_rbs[i], heads_on_rows(us[i], bd_pg)))
    for g in groups:
        vu_t = jnp.transpose(jnp.concatenate([vs[g], us[g]], axis=0))
        s_scr[g] = ss[g] * jnp.exp(w_lasts[g]) + jnp.where(bd_gg, _mm(vu_t, kbs[g]), 0.0)

    @pl.when(c == n_chunks - 1)
    def _():
        for i, (s, g) in enumerate(chains):
            for h in range(hpg):
                sout_ref[s, g * hpg + h] = s_scr[i, h * hd:(h + 1) * hd, h * hd:(h + 1) * hd]


def _rwkv_scan(r, lw, k, v, a, b, s0, chunk, n_par):
    n, d = r.shape
    n_seq, n_heads = s0.shape[0], s0.shape[1]
    t = n // n_seq
    n_chunks = t // chunk
    n_groups = d // MXU_W
    n_par = min(n_par, n_seq)
    assert n_seq % n_par == 0
    tile = pl.BlockSpec((n_par, chunk, d), lambda bi, c: (bi, c, 0))
    st = pl.BlockSpec((n_par, n_heads, HEAD_DIM, HEAD_DIM), lambda bi, c: (bi, 0, 0, 0))
    kern = functools.partial(_scan_kernel, chunk=chunk, n_chunks=n_chunks, n_groups=n_groups)
    y, s_new = pl.pallas_call(
        kern,
        grid=(n_seq // n_par, n_chunks),
        in_specs=[tile] * 6 + [st],
        out_specs=[tile, st],
        out_shape=[jax.ShapeDtypeStruct((n_seq, t, d), F32), jax.ShapeDtypeStruct(s0.shape, F32)],
        scratch_shapes=[pltpu.VMEM((n_par * n_groups, MXU_W, MXU_W), F32)],
        compiler_params=_params(("parallel", "arbitrary")),
        name=f"rwkv_scan_c{chunk}",
    )(*(z.reshape(n_seq, t, d) for z in (r, lw, k, v, a, b)), s0)
    return y.reshape(n, d), s_new


def _tm_post_kernel(y_ref, r_ref, k_ref, v_ref, g_ref, x_ref, wo_ref, gng_ref, gnb_ref, rk_ref,
                    lng_ref, lnb_ref, sel_ref, selt_ref, o_ref, *, alpha):
    sel, selt = sel_ref[...], selt_ref[...]
    y = y_ref[...]
    inv = 1.0 / HEAD_DIM
    yc = y - _head_sum(y, sel, selt) * inv
    var = _head_sum(yc * yc, sel, selt) * inv
    yn = yc * lax.rsqrt(var + GN_EPS) * gng_ref[...] + gnb_ref[...]
    bonus = _head_sum(r_ref[...] * k_ref[...] * rk_ref[...], sel, selt) * v_ref[...]
    z = ((yn + bonus) * g_ref[...]).astype(BF16)
    out = jnp.dot(z, wo_ref[...], preferred_element_type=F32)
    o_ref[...] = _layer_norm(alpha * x_ref[...] + out, lng_ref[...], lnb_ref[...])


def _tm_post(y, r, k, v, g, x, weights, alpha):
    n, d = x.shape
    tm = PRE_TILE
    tile = pl.BlockSpec((tm, d), lambda i: (i, 0))
    return pl.pallas_call(
        functools.partial(_tm_post_kernel, alpha=alpha),
        grid=(n // tm,),
        in_specs=[tile] * 6 + [_const_spec(w.shape) for w in weights],
        out_specs=tile,
        out_shape=jax.ShapeDtypeStruct((n, d), F32),
        compiler_params=_params(("parallel",)),
        name="tm_post",
    )(y, r, k, v, g, x, *weights)


def _proj_kernel(x_ref, *refs):
    n_out = len(refs) // 2
    xb = x_ref[...].astype(BF16)
    for w_ref, o_ref in zip(refs[:n_out], refs[n_out:]):
        o_ref[...] = jnp.dot(xb, w_ref[...], preferred_element_type=F32)


def _proj(x, ws, name):
    n, d = x.shape
    tm = TOKEN_TILE
    tile = pl.BlockSpec((tm, d), lambda i: (i, 0))
    return pl.pallas_call(
        _proj_kernel,
        grid=(n // tm,),
        in_specs=[tile] + [_const_spec(w.shape) for w in ws],
        out_specs=[pl.BlockSpec((tm, w.shape[1]), lambda i: (i, 0)) for w in ws],
        out_shape=[jax.ShapeDtypeStruct((n, w.shape[1]), F32) for w in ws],
        compiler_params=_params(("parallel",)),
        name=name,
    )(x, *ws)


def _oproj_kernel(o_ref, x_ref, w_ref, g_ref, b_ref, out_ref, *, alpha):
    sub = jnp.dot(o_ref[...].astype(BF16), w_ref[...], preferred_element_type=F32)
    out_ref[...] = _layer_norm(alpha * x_ref[...] + sub, g_ref[...], b_ref[...])


def _oproj_postnorm(o, x, w, g, b, alpha):
    n, d = x.shape
    tm = TOKEN_TILE
    tile = pl.BlockSpec((tm, d), lambda i: (i, 0))
    return pl.pallas_call(
        functools.partial(_oproj_kernel, alpha=alpha),
        grid=(n // tm,),
        in_specs=[tile, tile, _const_spec(w.shape), _const_spec(g.shape), _const_spec(b.shape)],
        out_specs=tile,
        out_shape=jax.ShapeDtypeStruct((n, d), F32),
        compiler_params=_params(("parallel",)),
        name="oproj_postnorm",
    )(o, x, w, g, b)


LOG2E = 1.4426950408889634


def _log2_survive(nz2):
    sign = jnp.uint32(0x80000000)
    neg_abs = lax.bitcast_convert_type(lax.bitcast_convert_type(nz2, jnp.uint32) | sign, F32)
    return jnp.minimum(nz2, 0.0) - jnp.log2(1.0 + jnp.exp2(neg_abs))


def _attn_prompt_kernel(bias_ref, q_ref, k_ref, v_ref, o_ref, *, tq, tk, n_sub):
    p = pl.program_id(1)
    qb = pl.program_id(2)
    hd = HEAD_DIM
    q = (q_ref[...] * (-SB_SCALE * LOG2E)).astype(BF16)
    bias = jnp.where(_iota((1, 2 * tk), 1) < tk, bias_ref[2 * p], bias_ref[2 * p + 1]) * (-LOG2E)
    row = _iota((tq, tk), 0)
    col = _iota((tq, tk), 1)
    head0_k = _iota((tk, LANES), 1) < hd
    r2 = _iota((tk, 2 * tk), 0)
    c2 = _iota((tk, 2 * tk), 1)
    suffix = jnp.where((c2 >= tk) | (r2 > c2), 1.0, 0.0).astype(BF16)

    kpi = n_sub * tk
    per_q = tq // tk
    n_phase = kpi // tq
    diag = (qb * tq) // kpi

    def heads_on_rows(blk):
        return jnp.concatenate([jnp.where(head0_k, blk, 0.0), jnp.where(head0_k, 0.0, blk)], axis=0).astype(BF16)

    def key_blocks(start, carry, phase):
        acc, car0, car1 = carry
        n_act = n_sub if phase is None else (phase + 1) * per_q
        subs = list(reversed(range(n_act)))
        st = [dict() for _ in subs]

        def stage(i, s):
            sub, c = subs[i], st[i]
            masked = phase is not None and sub >= phase * per_q
            if s == 0:
                kbd = heads_on_rows(k_ref[pl.ds(start + sub * tk, tk), :])
                c["nz"] = lax.dot_general(q, kbd, NT, preferred_element_type=F32) + bias
            elif s == 1:
                l1m = _log2_survive(c["nz"])
                if masked:
                    c["valid"] = jnp.concatenate([(col + (sub - phase * per_q) * tk) < row] * 2, axis=1)
                    l1m = jnp.where(c["valid"], l1m, 0.0)
                c["l1m"] = l1m
            elif s == 2:
                l1m = c["l1m"].astype(BF16)
                c["cs"] = [jnp.dot(l1m[:, h * tk:(h + 1) * tk], suffix, preferred_element_type=F32)
                           for h in range(2)]
            elif s == 3:
                nonlocal car0, car1
                surv = jnp.concatenate([c["cs"][0][:, :tk] + car0, c["cs"][1][:, :tk] + car1], axis=1)
                w = jnp.exp2(c["l1m"] - c["nz"] + surv)
                if masked:
                    w = jnp.where(c["valid"], w, 0.0)
                c["w"] = w.astype(BF16)
                car0 = car0 + c["cs"][0][:, tk:]
                car1 = car1 + c["cs"][1][:, tk:]
            else:
                nonlocal acc
                vbd = heads_on_rows(v_ref[pl.ds(start + sub * tk, tk), :])
                acc = acc + jnp.dot(c["w"], vbd, preferred_element_type=F32)

        n_stages = 5
        for t in range(n_act + n_stages - 1):
            for i in range(n_act):
                if 0 <= t - i < n_stages:
                    stage(i, t - i)
        return acc, car0, car1

    zero = jnp.zeros((tq, LANES), F32)
    if kpi == k_ref.shape[0]:
        carry = lax.switch(qb, [functools.partial(key_blocks, 0, (zero, zero, zero), ph)
                                for ph in range(n_phase)])
    else:
        diag_start = pl.multiple_of(diag * kpi, kpi)
        carry = lax.switch(qb % n_phase,
                           [functools.partial(key_blocks, diag_start, (zero, zero, zero), ph)
                            for ph in range(n_phase)])

        def body(i, carry):
            return key_blocks(pl.multiple_of((diag - i) * kpi, kpi), carry, None)

        carry = lax.fori_loop(1, diag + 1, body, carry)
    o_ref[...] = carry[0]


def _attn_prompt(q, k, v, bias, n_seq):
    n, d = q.shape
    t = n // n_seq
    tq, tk = ATT_Q, ATT_K
    n_sub = min(ATT_SUB, t // tk)
    assert tk == LANES and tq % tk == 0 and (n_sub * tk) % tq == 0 and t % (n_sub * tk) == 0
    n_blk = t // tq
    n_pairs = d // LANES
    return pl.pallas_call(
        functools.partial(_attn_prompt_kernel, tq=tq, tk=tk, n_sub=n_sub),
        grid_spec=pltpu.PrefetchScalarGridSpec(
            num_scalar_prefetch=1,
            grid=(n_seq, n_pairs, n_blk),
            in_specs=[
                pl.BlockSpec((tq, LANES), lambda bi, p, qb, bias_ref: (bi * n_blk + qb, p)),
                pl.BlockSpec((t, LANES), lambda bi, p, qb, bias_ref: (bi, p)),
                pl.BlockSpec((t, LANES), lambda bi, p, qb, bias_ref: (bi, p)),
            ],
            out_specs=pl.BlockSpec((tq, LANES), lambda bi, p, qb, bias_ref: (bi * n_blk + qb, p)),
        ),
        out_shape=jax.ShapeDtypeStruct((n, d), F32),
        compiler_params=_params(("parallel", "parallel", "arbitrary")),
        name="attn_prompt",
    )(bias, q, k, v)


def _attn_sample_kernel(pt_ref, q_ref, kn_ref, vn_ref, *refs, n_new, n_steps, n_pg):
    kc_refs, vc_refs = refs[:n_pg], refs[n_pg:2 * n_pg]
    biasc_ref, o_ref, qrep_scr, kpad_scr, vpad_scr, acc_scr, carry_scr = refs[2 * n_pg:]
    j = pl.program_id(1)
    page = LANES
    d = q_ref.shape[2]
    n_rows = (d // HEAD_DIM) * n_new
    own_head = (_iota((n_rows, d), 0) // n_new) == (_iota((n_rows, d), 1) // HEAD_DIM)
    r2 = _iota((page, 2 * page), 0)
    c2 = _iota((page, 2 * page), 1)
    suffix = jnp.where((c2 >= page) | (r2 > c2), 1.0, 0.0).astype(BF16)

    def visit(scores, values, n_blk, valid):
        nz = scores + jnp.concatenate([biasc_ref[...]] * n_blk, axis=1)
        l1m = _log2_survive(nz)
        if valid is not None:
            l1m = jnp.where(valid, l1m, 0.0)
        blks = [slice(i * page, (i + 1) * page) for i in range(n_blk)]
        cs = [jnp.dot(l1m[:, sl].astype(BF16), suffix, preferred_element_type=F32) for sl in blks]
        carry = carry_scr[...]
        ws = [None] * n_blk
        for i in reversed(range(n_blk)):
            ws[i] = jnp.exp2(l1m[:, blks[i]] - nz[:, blks[i]] + cs[i][:, :page] + carry)
            carry = carry + cs[i][:, page:]
        carry_scr[...] = carry
        if valid is not None:
            ws = [jnp.where(valid, ws[0], 0.0)]
        acc = acc_scr[...]
        for i in range(n_blk):
            acc = acc + values(i, ws[i].astype(BF16))
        acc_scr[...] = acc

    @pl.when(j == 0)
    def _():
        rr = _iota((n_rows, SAMPLE_PAD), 0)
        cc = _iota((n_rows, SAMPLE_PAD), 1)
        rep = jnp.where((rr % n_new) == cc, 1.0, 0.0).astype(BF16)
        qrep = jnp.dot(rep, (q_ref[0] * (-SB_SCALE * LOG2E)).astype(BF16), preferred_element_type=F32)
        qrep_scr[...] = jnp.where(own_head, qrep, 0.0).astype(BF16)
        kpad_scr[...] = jnp.zeros_like(kpad_scr)
        vpad_scr[...] = jnp.zeros_like(vpad_scr)
        kpad_scr[0:SAMPLE_PAD, :] = kn_ref[0].astype(BF16)
        vpad_scr[0:SAMPLE_PAD, :] = vn_ref[0].astype(BF16)
        acc_scr[...] = jnp.zeros_like(acc_scr)
        carry_scr[...] = jnp.zeros_like(carry_scr)
        valid = _iota((n_rows, page), 1) < (_iota((n_rows, page), 0) % n_new)
        scores = lax.dot_general(qrep_scr[...], kpad_scr[...], NT, preferred_element_type=F32)
        visit(scores, lambda i, w: jnp.dot(w, vpad_scr[...], preferred_element_type=F32), 1, valid)

    qrep = qrep_scr[...]
    scores = jnp.concatenate([jnp.dot(qrep, kc_refs[i][0].astype(BF16), preferred_element_type=F32)
                              for i in range(n_pg)], axis=1)

    def values(i, w):
        return lax.dot_general(w, vc_refs[i][0].astype(BF16), NT, preferred_element_type=F32)

    visit(scores, values, n_pg, None)

    @pl.when(j == n_steps - 1)
    def _():
        rr = _iota((SAMPLE_PAD, n_rows), 0)
        cc = _iota((SAMPLE_PAD, n_rows), 1)
        pick = jnp.where((cc % n_new) == rr, 1.0, 0.0)
        own = jnp.where(own_head, acc_scr[...], 0.0)
        o_ref[0] = jnp.dot(pick, own, preferred_element_type=F32, precision=HI)


def _attn_sample(q, k_new, v_new, cache_k, cache_v, page_table, bias_rows, n_new):
    n_seq, pad, d = q.shape
    n_pages = page_table.shape[1]
    page = cache_k.shape[2]
    assert page == LANES
    n_pg = min(PAGES_PER_STEP, n_pages)
    assert n_pages % n_pg == 0
    n_steps = n_pages // n_pg

    def page_spec(i):
        def index(bi, j, pt_ref):
            return (pt_ref[bi, n_pages - n_pg * (j + 1) + i], 0, 0)
        return pl.BlockSpec((1, d, page), index)

    row = pl.BlockSpec((1, pad, d), lambda bi, j, pt_ref: (bi, 0, 0))
    pages = [page_spec(i) for i in range(n_pg)]
    n_rows = bias_rows.shape[0]
    return pl.pallas_call(
        functools.partial(_attn_sample_kernel, n_new=n_new, n_steps=n_steps, n_pg=n_pg),
        grid_spec=pltpu.PrefetchScalarGridSpec(
            num_scalar_prefetch=1,
            grid=(n_seq, n_steps),
            in_specs=[row, row, row] + pages + pages
                     + [pl.BlockSpec((n_rows, LANES), lambda bi, j, pt_ref: (0, 0))],
            out_specs=row,
            scratch_shapes=[
                pltpu.VMEM((n_rows, d), BF16),
                pltpu.VMEM((page, d), BF16),
                pltpu.VMEM((page, d), BF16),
                pltpu.VMEM((n_rows, d), F32),
                pltpu.VMEM((n_rows, LANES), F32),
            ],
        ),
        out_shape=jax.ShapeDtypeStruct((n_seq, pad, d), F32),
        compiler_params=_params(("parallel", "arbitrary")),
        name="attn_sample",
    )(page_table, q, k_new, v_new, *([cache_k] * n_pg), *([cache_v] * n_pg), bias_rows)


def kernel(x_prompt, x_sample, cache_k, cache_v, state_wkv, state_shift, page_table, ln_g, ln_b, ffn_w_gate, ffn_w_up, ffn_w_down, tm_mu, tm_w0, tm_w1, tm_w2, tm_a0, tm_a1, tm_a2, tm_v0, tm_v1, tm_v2, tm_g1, tm_g2, tm_k_k, tm_k_a, tm_r_k, tm_w_r, tm_w_k, tm_w_v, tm_w_o, tm_gn_g, tm_gn_b, sb_w_k, sb_w_v, sb_w_q, sb_bias, sb_w_o):
    bp, sp, d = x_prompt.shape
    bs, ss, _ = x_sample.shape
    depth = ln_g.shape[0]
    n_a = tm_mu.shape[0]
    n_heads = d // HEAD_DIM
    alpha = (2 * depth) ** 0.25

    def vec(a):
        return a.reshape(1, -1).astype(F32)

    def bf(a):
        return a.astype(BF16)

    hid = jnp.arange(d) // HEAD_DIM
    sel = (hid[:, None] == jnp.arange(LANES)[None, :]).astype(BF16)
    sel_t = sel.T

    ffn_w = {}
    for li in range(depth):
        for j in range(2):
            ffn_w[li, j] = (ffn_w_gate[li, j], ffn_w_up[li, j], ffn_w_down[li, j],
                            vec(ln_g[li, 2 * j]), vec(ln_b[li, 2 * j]))
    pre_w, post_w = [], []
    for li in range(n_a):
        w = [tm_mu[li], bf(tm_w_r[li]), bf(tm_w_k[li]), bf(tm_w_v[li]),
             vec(tm_w0[li]), bf(tm_w1[li]), bf(tm_w2[li]), vec(tm_a0[li]), bf(tm_a1[li]), bf(tm_a2[li])]
        if li > 0:
            w += [vec(tm_v0[li - 1]), bf(tm_v1[li - 1]), bf(tm_v2[li - 1])]
        w += [bf(tm_g1[li]), bf(tm_g2[li]), vec(tm_k_k[li]), vec(tm_k_a[li]), sel, sel_t]
        pre_w.append(w)
        post_w.append([bf(tm_w_o[li]), vec(tm_gn_g[li]), vec(tm_gn_b[li]), vec(tm_r_k[li]),
                       vec(ln_g[li, 1]), vec(ln_b[li, 1]), sel, sel_t])
    wq = [bf(sb_w_q[j]) for j in range(depth - n_a)]
    wo = [bf(sb_w_o[j]) for j in range(depth - n_a)]
    wkv = [bf(sb_w_k), bf(sb_w_v)]
    n_phys, page = cache_k.shape[0], cache_k.shape[1]
    cache_k2 = jnp.transpose(cache_k, (0, 2, 3, 1)).reshape(n_phys, d, page)
    cache_v2 = jnp.transpose(cache_v, (0, 2, 3, 1)).reshape(n_phys, d, page)

    def trunk(x, n_seq, t, wkv_in, shift_in):
        fresh = shift_in is None
        n = n_seq * t
        new_wkv, new_shift = [], []
        v_first = None
        k_sh = v_sh = k_new = v_new = None

        def pad_rows(a):
            a = a.reshape(n_seq, t, d)
            return jnp.pad(a, ((0, 0), (0, SAMPLE_PAD - t), (0, 0)))

        for li in range(depth):
            x = _ffn_half(x, *ffn_w[li, 0], alpha)
            if li < n_a:
                x3 = x.reshape(n_seq, t, d)
                new_shift.append(x3[:, -1])
                if fresh:
                    x_shift = None
                else:
                    x_shift = jnp.concatenate([shift_in[li][:, None, :], x3[:, :-1]], axis=1).reshape(n, d)
                r, lw, k, v, a, b, g = _tm_pre(x, x_shift, v_first if li > 0 else None, pre_w[li], t)
                if li == 0:
                    v_first = v
                if fresh:
                    y, s_new = _rwkv_scan(r, lw, k, v, a, b, wkv_in[li], SCAN_CHUNK, SCAN_SEQS)
                else:
                    padded = (pad_rows(z).reshape(n_seq * SAMPLE_PAD, d) for z in (r, lw, k, v, a, b))
                    y, s_new = _rwkv_scan(*padded, wkv_in[li], SAMPLE_PAD, SCAN_SEQS_SAMPLE)
                    y = y.reshape(n_seq, SAMPLE_PAD, d)[:, :t].reshape(n, d)
                new_wkv.append(s_new)
                x = _tm_post(y, r, k, v, g, x, post_w[li], alpha)
            else:
                j = li - n_a
                (q,) = _proj(x, [wq[j]], "q_proj")
                if fresh:
                    o = _attn_prompt(q, k_sh, v_sh, sb_bias[j].astype(F32), n_seq)
                else:
                    nbias2 = sb_bias[j].astype(F32) * (-LOG2E)
                    bias_rows = jnp.broadcast_to(jnp.repeat(nbias2, t)[:, None], (n_heads * t, LANES))
                    o = _attn_sample(pad_rows(q), k_new, v_new, cache_k2, cache_v2, page_table, bias_rows, t)
                    o = o[:, :t].reshape(n, d)
                x = _oproj_postnorm(o, x, wo[j], vec(ln_g[li, 1]), vec(ln_b[li, 1]), alpha)
            x = _ffn_half(x, *ffn_w[li, 1], alpha)
            if li == n_a - 1:
                k_sh, v_sh = _proj(x, wkv, "kv_proj")
                if not fresh:
                    k_new, v_new = pad_rows(k_sh), pad_rows(v_sh)
        return x, jnp.stack(new_wkv), jnp.stack(new_shift), k_sh, v_sh

    zero_state = jnp.zeros((n_a, bp, n_heads, HEAD_DIM, HEAD_DIM), F32)
    y_p, wkv_p, shift_p, k_p, v_p = trunk(x_prompt.reshape(bp * sp, d), bp, sp, zero_state, None)
    y_s, wkv_s, shift_s, k_s, v_s = trunk(x_sample.reshape(bs * ss, d), bs, ss, state_wkv, state_shift)

    def paged(a):
        return a.reshape(bp, sp // PAGE_SIZE, PAGE_SIZE, n_heads, HEAD_DIM)

    return (y_p.reshape(bp, sp, d), y_s.reshape(bs, ss, d), wkv_p, shift_p, paged(k_p), paged(v_p),
            wkv_s, shift_s, k_s.reshape(bs, ss, n_heads, HEAD_DIM), v_s.reshape(bs, ss, n_heads, HEAD_DIM))
```

```python
import functools

import jax
import jax.numpy as jnp
from jax import lax
from jax.experimental import pallas as pl
from jax.experimental.pallas import tpu as pltpu

HEAD_DIM = 64
LANES = 128
SUBLANES = 8
MXU_W = 256
LN_EPS = 1e-5
GN_EPS = 64e-5
SB_SCALE = HEAD_DIM ** -0.5
PAGE_SIZE = 128

FFN_CHUNK = MXU_W
TOKEN_TILE = 512
PRE_TILE = 256
SCAN_CHUNK = 64
SCAN_SEQS = 4
SCAN_SEQS_SAMPLE = 8
SAMPLE_PAD = SUBLANES
ATT_Q = 256
ATT_K = LANES
ATT_SUB = 16
PAGES_PER_STEP = 16
VMEM_LIMIT = 56 * 1024 * 1024

F32 = jnp.float32
BF16 = jnp.bfloat16
HI = lax.Precision.HIGHEST
NT = (((1,), (1,)), ((), ()))


def _const_spec(shape):
    nd = len(shape)
    return pl.BlockSpec(shape, lambda *_: (0,) * nd, pipeline_mode=pl.Buffered(1))


def _params(sem):
    return pltpu.CompilerParams(dimension_semantics=sem, vmem_limit_bytes=VMEM_LIMIT)


def _iota(shape, axis):
    return lax.broadcasted_iota(jnp.int32, shape, axis)


def _layer_norm(y, g, b):
    mu = jnp.mean(y, axis=-1, keepdims=True)
    yc = y - mu
    var = jnp.mean(yc * yc, axis=-1, keepdims=True)
    return yc * lax.rsqrt(var + LN_EPS) * g + b


def _sigmoid(x):
    return 1.0 / (1.0 + jnp.exp(-x))


def _split_bf16(x):
    hi = x.astype(BF16)
    lo = (x - hi.astype(F32)).astype(BF16)
    return hi, lo


def _mm(a, b):
    return jnp.dot(a.astype(BF16), b.astype(BF16), preferred_element_type=F32)


def _mm_nt(a, b):
    return lax.dot_general(a.astype(BF16), b.astype(BF16), NT, preferred_element_type=F32)


def _head_sum(q, sel, sel_t):
    hi, lo = _split_bf16(q)
    s = jnp.dot(hi, sel, preferred_element_type=F32) + jnp.dot(lo, sel, preferred_element_type=F32)
    hi, lo = _split_bf16(s)
    return jnp.dot(hi, sel_t, preferred_element_type=F32) + jnp.dot(lo, sel_t, preferred_element_type=F32)


def _ffn_kernel(x_ref, wg_ref, wu_ref, wd_ref, g_ref, b_ref, o_ref, acc_ref, *, alpha, n_chunks):
    x = x_ref[...]
    xb = x.astype(BF16)
    acc_ref[...] = jnp.zeros_like(acc_ref)

    for c in range(n_chunks):
        cols = slice(c * FFN_CHUNK, (c + 1) * FFN_CHUNK)
        gate = jnp.dot(xb, wg_ref[:, cols].astype(BF16), preferred_element_type=F32)
        up = jnp.dot(xb, wu_ref[:, cols].astype(BF16), preferred_element_type=F32)
        h = (gate * _sigmoid(gate) * up).astype(BF16)
        acc_ref[...] += jnp.dot(h, wd_ref[cols, :].astype(BF16), preferred_element_type=F32)
    y = alpha * x + 0.5 * acc_ref[...]
    o_ref[...] = _layer_norm(y, g_ref[...], b_ref[...])


def _ffn_half(x, wg, wu, wd, li, j, g, b, alpha):
    n, d = x.shape
    f = wg.shape[-1]
    n_chunks = f // FFN_CHUNK
    tm = TOKEN_TILE
    kern = functools.partial(_ffn_kernel, alpha=alpha, n_chunks=n_chunks)

    def weight(shape):
        return pl.BlockSpec((None, None) + shape, lambda i: (li, j, 0, 0), pipeline_mode=pl.Buffered(1))

    return pl.pallas_call(
        kern,
        grid=(n // tm,),
        in_specs=[
            pl.BlockSpec((tm, d), lambda i: (i, 0)),
            weight((d, f)), weight((d, f)), weight((f, d)),
            _const_spec(g.shape), _const_spec(b.shape),
        ],
        out_specs=pl.BlockSpec((tm, d), lambda i: (i, 0)),
        out_shape=jax.ShapeDtypeStruct((n, d), F32),
        scratch_shapes=[pltpu.VMEM((tm, d), F32)],
        compiler_params=_params(("parallel",)),
        name="ffn_half",
    )(x, wg, wu, wd, g, b)


def _tm_pre_kernel(*refs, has_vres, seq_len, tile):
    x_ref, shift_ref = refs[0], refs[1]
    refs = refs[2:]
    if has_vres:
        vf_ref, refs = refs[0], refs[1:]
    (mu_ref, wr_ref, wk_ref, wv_ref, w0_ref, w1_ref, w2_ref, a0_ref, a1_ref, a2_ref) = refs[:10]
    refs = refs[10:]
    if has_vres:
        (v0_ref, v1_ref, v2_ref), refs = refs[:3], refs[3:]
    (g1_ref, g2_ref, kk_ref, ka_ref, sel_ref, selt_ref,
     r_out, lw_out, k_out, v_out, a_out, b_out, g_out) = refs
    x = x_ref[...]
    if seq_len is None:
        xs = shift_ref[...]
    else:
        starts_sequence = (pl.program_id(0) * tile) % seq_len == 0
        first = jnp.where(starts_sequence, 0.0, shift_ref[SUBLANES - 1:SUBLANES, :])
        xs = jnp.where(_iota(x.shape, 0) == 0, first, pltpu.roll(x, 1, 0))
    xx = xs - x

    def mix(i):
        return (x + xx * mu_ref[i:i + 1, :]).astype(BF16)

    def mm(a, w_ref):
        return jnp.dot(a, w_ref[...], preferred_element_type=F32)

    xr, xw, xk, xv, xa, xg = (mix(i) for i in range(6))
    r_out[...] = mm(xr, wr_ref)
    w_in = w0_ref[...] + mm(jnp.tanh(mm(xw, w1_ref)).astype(BF16), w2_ref)
    softplus = jnp.maximum(-w_in, 0.0) + jnp.log1p(jnp.exp(-jnp.abs(w_in)))
    lw_out[...] = -jnp.exp(-softplus - 0.5)
    k = mm(xk, wk_ref)
    v = mm(xv, wv_ref)
    if has_vres:
        gate = _sigmoid(v0_ref[...] + mm(mm(xv, v1_ref).astype(BF16), v2_ref))
        v = v + (vf_ref[...] - v) * gate
    v_out[...] = v
    a = _sigmoid(a0_ref[...] + mm(mm(xa, a1_ref).astype(BF16), a2_ref))
    g_out[...] = mm(_sigmoid(mm(xg, g1_ref)).astype(BF16), g2_ref)
    kk = k * kk_ref[...]
    norm = jnp.sqrt(_head_sum(kk * kk, sel_ref[...], selt_ref[...]))
    kk = kk / jnp.maximum(norm, 1e-12)
    k_out[...] = k * (1.0 + (a - 1.0) * ka_ref[...])
    a_out[...] = -kk
    b_out[...] = kk * a


def _tm_pre(x, x_shift, v_first, weights, seq_len):
    n, d = x.shape
    tm = PRE_TILE
    has_vres = v_first is not None
    tile = pl.BlockSpec((tm, d), lambda i: (i, 0))
    if x_shift is None:
        per = tm // SUBLANES
        shift_in = x
        shift_spec = pl.BlockSpec((SUBLANES, d), lambda i: (jnp.maximum(i * per - 1, 0), 0))
    else:
        seq_len = None
        shift_in = x_shift
        shift_spec = tile
    acts = [x, shift_in] + ([v_first] if has_vres else [])
    kern = functools.partial(_tm_pre_kernel, has_vres=has_vres, seq_len=seq_len, tile=tm)
    out = jax.ShapeDtypeStruct((n, d), F32)
    return pl.pallas_call(
        kern,
        grid=(n // tm,),
        in_specs=[tile, shift_spec] + [tile] * has_vres + [_const_spec(w.shape) for w in weights],
        out_specs=[tile] * 7,
        out_shape=[out] * 7,
        compiler_params=_params(("parallel",)),
        name="tm_pre_vres" if has_vres else "tm_pre",
    )(*acts, *weights)


def _scan_kernel(r_ref, lw_ref, k_ref, v_ref, a_ref, b_ref, s0_ref, y_ref, sout_ref, s_scr,
                 *, chunk, n_chunks, n_groups):
    c = pl.program_id(1)
    hd = HEAD_DIM
    gw = MXU_W
    hpg = gw // hd
    pw = hpg * chunk

    n_par = r_ref.shape[0]
    chains = [(s, g) for s in range(n_par) for g in range(n_groups)]

    @pl.when(c == 0)
    def _():
        s_scr[...] = jnp.zeros_like(s_scr)
        for i, (s, g) in enumerate(chains):
            for h in range(hpg):
                s_scr[i, h * hd:(h + 1) * hd, h * hd:(h + 1) * hd] = s0_ref[s, g * hpg + h]

    row_p = _iota((chunk, pw), 0)
    j_p = _iota((chunk, pw), 1) % chunk
    strict = j_p < row_p
    incl = j_p <= row_p
    tri = jnp.where(_iota((chunk, chunk), 1) <= _iota((chunk, chunk), 0), 1.0, 0.0).astype(BF16)
    bd_pp = (_iota((pw, pw), 0) // chunk) == (_iota((pw, pw), 1) // chunk)
    bd_pg = (_iota((pw, gw), 0) // chunk) == (_iota((pw, gw), 1) // hd)
    bd_gg = (_iota((gw, gw), 0) // hd) == (_iota((gw, gw), 1) // hd)

    def heads_on_rows(x, mask):
        return jnp.where(mask, jnp.concatenate([x] * hpg, axis=0), 0.0).astype(BF16)

    groups = range(len(chains))
    sls = [slice(g * gw, (g + 1) * gw) for _, g in chains]
    vs, kbs, w_lasts, ars, r_bs, r_ks = [], [], [], [], [], []
    for (s, _), sl in zip(chains, sls):
        r, lw, k, v, a, b = (ref[s, :, sl] for ref in (r_ref, lw_ref, k_ref, v_ref, a_ref, b_ref))
        hi = lw.astype(BF16)
        rem = lw - hi.astype(F32)
        mid = rem.astype(BF16)
        lo = (rem - mid.astype(F32)).astype(BF16)
        cum = (jnp.dot(tri, hi, preferred_element_type=F32) + jnp.dot(tri, mid, preferred_element_type=F32)
               + jnp.dot(tri, lo, preferred_element_type=F32))
        w_last = cum[chunk - 1:chunk, :]
        w_neg = jnp.exp(-cum)
        w_rest = jnp.exp(w_last - cum)
        vs.append(v)
        w_lasts.append(w_last)
        kbs.append(jnp.concatenate([k * w_rest, b * w_rest], axis=0))
        ars.append(jnp.concatenate([a * jnp.exp(cum - lw), r * jnp.exp(cum)], axis=0).astype(BF16))
        r_bs.append(heads_on_rows(b * w_neg, bd_pg))
        r_ks.append(heads_on_rows(k * w_neg, bd_pg))

    ss = [s_scr[g] for g in groups]
    g_bs = [_mm_nt(ars[g], r_bs[g]) for g in groups]
    g_ks = [_mm_nt(ars[g], r_ks[g]) for g in groups]
    from_state = [_mm_nt(ars[g], ss[g]) for g in groups]
    v_bds = [heads_on_rows(vs[g], bd_pg) for g in groups]
    a_abs = [jnp.where(strict, g_bs[g][:chunk], 0.0) for g in groups]
    a_rbs = [jnp.where(incl, g_bs[g][chunk:], 0.0) for g in groups]
    a_aks = [jnp.where(strict, g_ks[g][:chunk], 0.0) for g in groups]
    a_rks = [jnp.where(incl, g_ks[g][chunk:], 0.0) for g in groups]
    us = [from_state[g][:chunk] + _mm(a_aks[g], v_bds[g]) for g in groups]
    powers = a_abs
    span = 1
    while True:
        us = [us[g] + _mm(powers[g], heads_on_rows(us[g], bd_pg)) for g in groups]
        span *= 2
        if span >= chunk:
            break
        powers = [_mm(powers[g], heads_on_rows(powers[g], bd_pp)) for g in groups]
    for i, (s, _) in enumerate(chains):
        y_ref[s, :, sls[i]] = (from_state[i][chunk:] + _mm(a_rks[i], v_bds[i])
                               + _mm(a_rbs[i], heads_on_rows(us[i], bd_pg)))
    for g in groups:
        vu_t = jnp.transpose(jnp.concatenate([vs[g], us[g]], axis=0))
        s_scr[g] = ss[g] * jnp.exp(w_lasts[g]) + jnp.where(bd_gg, _mm(vu_t, kbs[g]), 0.0)

    @pl.when(c == n_chunks - 1)
    def _():
        for i, (s, g) in enumerate(chains):
            for h in range(hpg):
                sout_ref[s, g * hpg + h] = s_scr[i, h * hd:(h + 1) * hd, h * hd:(h + 1) * hd]


def _rwkv_scan(r, lw, k, v, a, b, s0, chunk, n_par):
    n, d = r.shape
    n_seq, n_heads = s0.shape[0], s0.shape[1]
    t = n // n_seq
    n_chunks = t // chunk
    n_groups = d // MXU_W
    n_par = min(n_par, n_seq)
    assert n_seq % n_par == 0
    tile = pl.BlockSpec((n_par, chunk, d), lambda bi, c: (bi, c, 0))
    st = pl.BlockSpec((n_par, n_heads, HEAD_DIM, HEAD_DIM), lambda bi, c: (bi, 0, 0, 0))
    kern = functools.partial(_scan_kernel, chunk=chunk, n_chunks=n_chunks, n_groups=n_groups)
    y, s_new = pl.pallas_call(
        kern,
        grid=(n_seq // n_par, n_chunks),
        in_specs=[tile] * 6 + [st],
        out_specs=[tile, st],
        out_shape=[jax.ShapeDtypeStruct((n_seq, t, d), F32), jax.ShapeDtypeStruct(s0.shape, F32)],
        scratch_shapes=[pltpu.VMEM((n_par * n_groups, MXU_W, MXU_W), F32)],
        compiler_params=_params(("parallel", "arbitrary")),
        name=f"rwkv_scan_c{chunk}",
    )(*(z.reshape(n_seq, t, d) for z in (r, lw, k, v, a, b)), s0)
    return y.reshape(n, d), s_new


def _tm_post_kernel(y_ref, r_ref, k_ref, v_ref, g_ref, x_ref, wo_ref, gng_ref, gnb_ref, rk_ref,
                    lng_ref, lnb_ref, sel_ref, selt_ref, o_ref, *, alpha):
    sel, selt = sel_ref[...], selt_ref[...]
    y = y_ref[...]
    inv = 1.0 / HEAD_DIM
    yc = y - _head_sum(y, sel, selt) * inv
    var = _head_sum(yc * yc, sel, selt) * inv
    yn = yc * lax.rsqrt(var + GN_EPS) * gng_ref[...] + gnb_ref[...]
    bonus = _head_sum(r_ref[...] * k_ref[...] * rk_ref[...], sel, selt) * v_ref[...]
    z = ((yn + bonus) * g_ref[...]).astype(BF16)
    out = jnp.dot(z, wo_ref[...], preferred_element_type=F32)
    o_ref[...] = _layer_norm(alpha * x_ref[...] + out, lng_ref[...], lnb_ref[...])


def _tm_post(y, r, k, v, g, x, weights, alpha):
    n, d = x.shape
    tm = PRE_TILE
    tile = pl.BlockSpec((tm, d), lambda i: (i, 0))
    return pl.pallas_call(
        functools.partial(_tm_post_kernel, alpha=alpha),
        grid=(n // tm,),
        in_specs=[tile] * 6 + [_const_spec(w.shape) for w in weights],
        out_specs=tile,
        out_shape=jax.ShapeDtypeStruct((n, d), F32),
        compiler_params=_params(("parallel",)),
        name="tm_post",
    )(y, r, k, v, g, x, *weights)


def _proj_kernel(x_ref, *refs):
    n_out = len(refs) // 2
    xb = x_ref[...].astype(BF16)
    for w_ref, o_ref in zip(refs[:n_out], refs[n_out:]):
        o_ref[...] = jnp.dot(xb, w_ref[...], preferred_element_type=F32)


def _proj(x, ws, name):
    n, d = x.shape
    tm = TOKEN_TILE
    tile = pl.BlockSpec((tm, d), lambda i: (i, 0))
    return pl.pallas_call(
        _proj_kernel,
        grid=(n // tm,),
        in_specs=[tile] + [_const_spec(w.shape) for w in ws],
        out_specs=[pl.BlockSpec((tm, w.shape[1]), lambda i: (i, 0)) for w in ws],
        out_shape=[jax.ShapeDtypeStruct((n, w.shape[1]), F32) for w in ws],
        compiler_params=_params(("parallel",)),
        name=name,
    )(x, *ws)


def _oproj_kernel(o_ref, x_ref, w_ref, g_ref, b_ref, out_ref, *, alpha):
    sub = jnp.dot(o_ref[...].astype(BF16), w_ref[...], preferred_element_type=F32)
    out_ref[...] = _layer_norm(alpha * x_ref[...] + sub, g_ref[...], b_ref[...])


def _oproj_postnorm(o, x, w, g, b, alpha):
    n, d = x.shape
    tm = TOKEN_TILE
    tile = pl.BlockSpec((tm, d), lambda i: (i, 0))
    return pl.pallas_call(
        functools.partial(_oproj_kernel, alpha=alpha),
        grid=(n // tm,),
        in_specs=[tile, tile, _const_spec(w.shape), _const_spec(g.shape), _const_spec(b.shape)],
        out_specs=tile,
        out_shape=jax.ShapeDtypeStruct((n, d), F32),
        compiler_params=_params(("parallel",)),
        name="oproj_postnorm",
    )(o, x, w, g, b)


LOG2E = 1.4426950408889634


def _log2_survive(nz2):
    sign = jnp.uint32(0x80000000)
    neg_abs = lax.bitcast_convert_type(lax.bitcast_convert_type(nz2, jnp.uint32) | sign, F32)
    return jnp.minimum(nz2, 0.0) - jnp.log2(1.0 + jnp.exp2(neg_abs))


def _attn_prompt_kernel(bias_ref, q_ref, k_ref, v_ref, o_ref, *, tq, tk, n_sub):
    p = pl.program_id(1)
    qb = pl.program_id(2)
    hd = HEAD_DIM
    q = (q_ref[...] * (-SB_SCALE * LOG2E)).astype(BF16)
    bias = jnp.where(_iota((1, 2 * tk), 1) < tk, bias_ref[2 * p], bias_ref[2 * p + 1]) * (-LOG2E)
    row = _iota((tq, tk), 0)
    col = _iota((tq, tk), 1)
    head0_k = _iota((tk, LANES), 1) < hd
    r2 = _iota((tk, 2 * tk), 0)
    c2 = _iota((tk, 2 * tk), 1)
    suffix = jnp.where((c2 >= tk) | (r2 > c2), 1.0, 0.0).astype(BF16)

    kpi = n_sub * tk
    per_q = tq // tk
    n_phase = kpi // tq
    diag = (qb * tq) // kpi

    def heads_on_rows(blk):
        return jnp.concatenate([jnp.where(head0_k, blk, 0.0), jnp.where(head0_k, 0.0, blk)], axis=0).astype(BF16)

    def key_blocks(start, carry, phase):
        acc, car0, car1 = carry
        n_act = n_sub if phase is None else (phase + 1) * per_q
        subs = list(reversed(range(n_act)))
        st = [dict() for _ in subs]

        def stage(i, s):
            sub, c = subs[i], st[i]
            masked = phase is not None and sub >= phase * per_q
            if s == 0:
                kbd = heads_on_rows(k_ref[pl.ds(start + sub * tk, tk), :])
                c["nz"] = lax.dot_general(q, kbd, NT, preferred_element_type=F32) + bias
            elif s == 1:
                l1m = _log2_survive(c["nz"])
                if masked:
                    c["valid"] = jnp.concatenate([(col + (sub - phase * per_q) * tk) < row] * 2, axis=1)
                    l1m = jnp.where(c["valid"], l1m, 0.0)
                c["l1m"] = l1m
            elif s == 2:
                l1m = c["l1m"].astype(BF16)
                c["cs"] = [jnp.dot(l1m[:, h * tk:(h + 1) * tk], suffix, preferred_element_type=F32)
                           for h in range(2)]
            elif s == 3:
                nonlocal car0, car1
                surv = jnp.concatenate([c["cs"][0][:, :tk] + car0, c["cs"][1][:, :tk] + car1], axis=1)
                w = jnp.exp2(c["l1m"] - c["nz"] + surv)
                if masked:
                    w = jnp.where(c["valid"], w, 0.0)
                c["w"] = w.astype(BF16)
                car0 = car0 + c["cs"][0][:, tk:]
                car1 = car1 + c["cs"][1][:, tk:]
            else:
                nonlocal acc
                vbd = heads_on_rows(v_ref[pl.ds(start + sub * tk, tk), :])
                acc = acc + jnp.dot(c["w"], vbd, preferred_element_type=F32)

        n_stages = 5
        for t in range(n_act + n_stages - 1):
            for i in range(n_act):
                if 0 <= t - i < n_stages:
                    stage(i, t - i)
        return acc, car0, car1

    zero = jnp.zeros((tq, LANES), F32)
    if kpi == k_ref.shape[0]:
        carry = lax.switch(qb, [functools.partial(key_blocks, 0, (zero, zero, zero), ph)
                                for ph in range(n_phase)])
    else:
        diag_start = pl.multiple_of(diag * kpi, kpi)
        carry = lax.switch(qb % n_phase,
                           [functools.partial(key_blocks, diag_start, (zero, zero, zero), ph)
                            for ph in range(n_phase)])

        def body(i, carry):
            return key_blocks(pl.multiple_of((diag - i) * kpi, kpi), carry, None)

        carry = lax.fori_loop(1, diag + 1, body, carry)
    o_ref[...] = carry[0]


def _attn_prompt(q, k, v, bias, n_seq):
    n, d = q.shape
    t = n // n_seq
    tq, tk = ATT_Q, ATT_K
    n_sub = min(ATT_SUB, t // tk)
    assert tk == LANES and tq % tk == 0 and (n_sub * tk) % tq == 0 and t % (n_sub * tk) == 0
    n_blk = t // tq
    n_pairs = d // LANES
    return pl.pallas_call(
        functools.partial(_attn_prompt_kernel, tq=tq, tk=tk, n_sub=n_sub),
        grid_spec=pltpu.PrefetchScalarGridSpec(
            num_scalar_prefetch=1,
            grid=(n_seq, n_pairs, n_blk),
            in_specs=[
                pl.BlockSpec((tq, LANES), lambda bi, p, qb, bias_ref: (bi * n_blk + qb, p)),
                pl.BlockSpec((t, LANES), lambda bi, p, qb, bias_ref: (bi, p)),
                pl.BlockSpec((t, LANES), lambda bi, p, qb, bias_ref: (bi, p)),
            ],
            out_specs=pl.BlockSpec((tq, LANES), lambda bi, p, qb, bias_ref: (bi * n_blk + qb, p)),
        ),
        out_shape=jax.ShapeDtypeStruct((n, d), F32),
        compiler_params=_params(("parallel", "parallel", "arbitrary")),
        name="attn_prompt",
    )(bias, q, k, v)


def _attn_sample_kernel(pt_ref, q_ref, kn_ref, vn_ref, *refs, n_new, n_steps, n_pg):
    kc_refs, vc_refs = refs[:n_pg], refs[n_pg:2 * n_pg]
    biasc_ref, o_ref, qrep_scr, kpad_scr, vpad_scr, acc_scr, carry_scr = refs[2 * n_pg:]
    j = pl.program_id(1)
    page = LANES
    d = q_ref.shape[2]
    n_rows = (d // HEAD_DIM) * n_new
    own_head = (_iota((n_rows, d), 0) // n_new) == (_iota((n_rows, d), 1) // HEAD_DIM)
    r2 = _iota((page, 2 * page), 0)
    c2 = _iota((page, 2 * page), 1)
    suffix = jnp.where((c2 >= page) | (r2 > c2), 1.0, 0.0).astype(BF16)

    def visit(scores, values, n_blk, valid):
        nz = scores + jnp.concatenate([biasc_ref[...]] * n_blk, axis=1)
        l1m = _log2_survive(nz)
        if valid is not None:
            l1m = jnp.where(valid, l1m, 0.0)
        blks = [slice(i * page, (i + 1) * page) for i in range(n_blk)]
        cs = [jnp.dot(l1m[:, sl].astype(BF16), suffix, preferred_element_type=F32) for sl in blks]
        carry = carry_scr[...]
        ws = [None] * n_blk
        for i in reversed(range(n_blk)):
            ws[i] = jnp.exp2(l1m[:, blks[i]] - nz[:, blks[i]] + cs[i][:, :page] + carry)
            carry = carry + cs[i][:, page:]
        carry_scr[...] = carry
        if valid is not None:
            ws = [jnp.where(valid, ws[0], 0.0)]
        acc = acc_scr[...]
        for i in range(n_blk):
            acc = acc + values(i, ws[i].astype(BF16))
        acc_scr[...] = acc

    @pl.when(j == 0)
    def _():
        rr = _iota((n_rows, SAMPLE_PAD), 0)
        cc = _iota((n_rows, SAMPLE_PAD), 1)
        rep = jnp.where((rr % n_new) == cc, 1.0, 0.0).astype(BF16)
        qrep = jnp.dot(rep, (q_ref[0] * (-SB_SCALE * LOG2E)).astype(BF16), preferred_element_type=F32)
        qrep_scr[...] = jnp.where(own_head, qrep, 0.0).astype(BF16)
        kpad_scr[...] = jnp.zeros_like(kpad_scr)
        vpad_scr[...] = jnp.zeros_like(vpad_scr)
        kpad_scr[0:SAMPLE_PAD, :] = kn_ref[0].astype(BF16)
        vpad_scr[0:SAMPLE_PAD, :] = vn_ref[0].astype(BF16)
        acc_scr[...] = jnp.zeros_like(acc_scr)
        carry_scr[...] = jnp.zeros_like(carry_scr)
        valid = _iota((n_rows, page), 1) < (_iota((n_rows, page), 0) % n_new)
        scores = lax.dot_general(qrep_scr[...], kpad_scr[...], NT, preferred_element_type=F32)
        visit(scores, lambda i, w: jnp.dot(w, vpad_scr[...], preferred_element_type=F32), 1, valid)

    qrep = qrep_scr[...]
    scores = jnp.concatenate([jnp.dot(qrep, kc_refs[i][0].astype(BF16), preferred_element_type=F32)
                              for i in range(n_pg)], axis=1)

    def values(i, w):
        return lax.dot_general(w, vc_refs[i][0].astype(BF16), NT, preferred_element_type=F32)

    visit(scores, values, n_pg, None)

    @pl.when(j == n_steps - 1)
    def _():
        rr = _iota((SAMPLE_PAD, n_rows), 0)
        cc = _iota((SAMPLE_PAD, n_rows), 1)
        pick = jnp.where((cc % n_new) == rr, 1.0, 0.0)
        own = jnp.where(own_head, acc_scr[...], 0.0)
        o_ref[0] = jnp.dot(pick, own, preferred_element_type=F32, precision=HI)


def _attn_sample(q, k_new, v_new, cache_k, cache_v, page_table, bias_rows, n_new):
    n_seq, pad, d = q.shape
    n_pages = page_table.shape[1]
    page = cache_k.shape[2]
    assert page == LANES
    n_pg = min(PAGES_PER_STEP, n_pages)
    assert n_pages % n_pg == 0
    n_steps = n_pages // n_pg

    def page_spec(i):
        def index(bi, j, pt_ref):
            return (pt_ref[bi, n_pages - n_pg * (j + 1) + i], 0, 0)
        return pl.BlockSpec((1, d, page), index)

    row = pl.BlockSpec((1, pad, d), lambda bi, j, pt_ref: (bi, 0, 0))
    pages = [page_spec(i) for i in range(n_pg)]
    n_rows = bias_rows.shape[0]
    return pl.pallas_call(
        functools.partial(_attn_sample_kernel, n_new=n_new, n_steps=n_steps, n_pg=n_pg),
        grid_spec=pltpu.PrefetchScalarGridSpec(
            num_scalar_prefetch=1,
            grid=(n_seq, n_steps),
            in_specs=[row, row, row] + pages + pages
                     + [pl.BlockSpec((n_rows, LANES), lambda bi, j, pt_ref: (0, 0))],
            out_specs=row,
            scratch_shapes=[
                pltpu.VMEM((n_rows, d), BF16),
                pltpu.VMEM((page, d), BF16),
                pltpu.VMEM((page, d), BF16),
                pltpu.VMEM((n_rows, d), F32),
                pltpu.VMEM((n_rows, LANES), F32),
            ],
        ),
        out_shape=jax.ShapeDtypeStruct((n_seq, pad, d), F32),
        compiler_params=_params(("parallel", "arbitrary")),
        name="attn_sample",
    )(page_table, q, k_new, v_new, *([cache_k] * n_pg), *([cache_v] * n_pg), bias_rows)


def kernel(x_prompt, x_sample, cache_k, cache_v, state_wkv, state_shift, page_table, ln_g, ln_b, ffn_w_gate, ffn_w_up, ffn_w_down, tm_mu, tm_w0, tm_w1, tm_w2, tm_a0, tm_a1, tm_a2, tm_v0, tm_v1, tm_v2, tm_g1, tm_g2, tm_k_k, tm_k_a, tm_r_k, tm_w_r, tm_w_k, tm_w_v, tm_w_o, tm_gn_g, tm_gn_b, sb_w_k, sb_w_v, sb_w_q, sb_bias, sb_w_o):
    bp, sp, d = x_prompt.shape
    bs, ss, _ = x_sample.shape
    depth = ln_g.shape[0]
    n_a = tm_mu.shape[0]
    n_heads = d // HEAD_DIM
    alpha = (2 * depth) ** 0.25

    def vec(a):
        return a.reshape(1, -1).astype(F32)

    def bf(a):
        return a.astype(BF16)

    hid = jnp.arange(d) // HEAD_DIM
    sel = (hid[:, None] == jnp.arange(LANES)[None, :]).astype(BF16)
    sel_t = sel.T

    def ffn(x, li, j):
        return _ffn_half(x, ffn_w_gate, ffn_w_up, ffn_w_down, li, j,
                         vec(ln_g[li, 2 * j]), vec(ln_b[li, 2 * j]), alpha)

    pre_w, post_w = [], []
    for li in range(n_a):
        w = [tm_mu[li], bf(tm_w_r[li]), bf(tm_w_k[li]), bf(tm_w_v[li]),
             vec(tm_w0[li]), bf(tm_w1[li]), bf(tm_w2[li]), vec(tm_a0[li]), bf(tm_a1[li]), bf(tm_a2[li])]
        if li > 0:
            w += [vec(tm_v0[li - 1]), bf(tm_v1[li - 1]), bf(tm_v2[li - 1])]
        w += [bf(tm_g1[li]), bf(tm_g2[li]), vec(tm_k_k[li]), vec(tm_k_a[li]), sel, sel_t]
        pre_w.append(w)
        post_w.append([bf(tm_w_o[li]), vec(tm_gn_g[li]), vec(tm_gn_b[li]), vec(tm_r_k[li]),
                       vec(ln_g[li, 1]), vec(ln_b[li, 1]), sel, sel_t])
    wq = [bf(sb_w_q[j]) for j in range(depth - n_a)]
    wo = [bf(sb_w_o[j]) for j in range(depth - n_a)]
    wkv = [bf(sb_w_k), bf(sb_w_v)]
    n_phys, page = cache_k.shape[0], cache_k.shape[1]
    cache_k2 = jnp.transpose(cache_k, (0, 2, 3, 1)).reshape(n_phys, d, page)
    cache_v2 = jnp.transpose(cache_v, (0, 2, 3, 1)).reshape(n_phys, d, page)

    def trunk(x, n_seq, t, wkv_in, shift_in):
        fresh = shift_in is None
        n = n_seq * t
        new_wkv, new_shift = [], []
        v_first = None
        k_sh = v_sh = k_new = v_new = None

        def pad_rows(a):
            a = a.reshape(n_seq, t, d)
            return jnp.pad(a, ((0, 0), (0, SAMPLE_PAD - t), (0, 0)))

        for li in range(depth):
            x = ffn(x, li, 0)
            if li < n_a:
                x3 = x.reshape(n_seq, t, d)
                new_shift.append(x3[:, -1])
                if fresh:
                    x_shift = None
                else:
                    x_shift = jnp.concatenate([shift_in[li][:, None, :], x3[:, :-1]], axis=1).reshape(n, d)
                r, lw, k, v, a, b, g = _tm_pre(x, x_shift, v_first if li > 0 else None, pre_w[li], t)
                if li == 0:
                    v_first = v
                if fresh:
                    y, s_new = _rwkv_scan(r, lw, k, v, a, b, wkv_in[li], SCAN_CHUNK, SCAN_SEQS)
                else:
                    padded = (pad_rows(z).reshape(n_seq * SAMPLE_PAD, d) for z in (r, lw, k, v, a, b))
                    y, s_new = _rwkv_scan(*padded, wkv_in[li], SAMPLE_PAD, SCAN_SEQS_SAMPLE)
                    y = y.reshape(n_seq, SAMPLE_PAD, d)[:, :t].reshape(n, d)
                new_wkv.append(s_new)
                x = _tm_post(y, r, k, v, g, x, post_w[li], alpha)
            else:
                j = li - n_a
                (q,) = _proj(x, [wq[j]], "q_proj")
                if fresh:
                    o = _attn_prompt(q, k_sh, v_sh, sb_bias[j].astype(F32), n_seq)
                else:
                    nbias2 = sb_bias[j].astype(F32) * (-LOG2E)
                    bias_rows = jnp.broadcast_to(jnp.repeat(nbias2, t)[:, None], (n_heads * t, LANES))
                    o = _attn_sample(pad_rows(q), k_new, v_new, cache_k2, cache_v2, page_table, bias_rows, t)
                    o = o[:, :t].reshape(n, d)
                x = _oproj_postnorm(o, x, wo[j], vec(ln_g[li, 1]), vec(ln_b[li, 1]), alpha)
            x = ffn(x, li, 1)
            if li == n_a - 1:
                k_sh, v_sh = _proj(x, wkv, "kv_proj")
                if not fresh:
                    k_new, v_new = pad_rows(k_sh), pad_rows(v_sh)
        return x, jnp.stack(new_wkv), jnp.stack(new_shift), k_sh, v_sh

    zero_state = jnp.zeros((n_a, bp, n_heads, HEAD_DIM, HEAD_DIM), F32)
    y_p, wkv_p, shift_p, k_p, v_p = trunk(x_prompt.reshape(bp * sp, d), bp, sp, zero_state, None)
    y_s, wkv_s, shift_s, k_s, v_s = trunk(x_sample.reshape(bs * ss, d), bs, ss, state_wkv, state_shift)

    def paged(a):
        return a.reshape(bp, sp // PAGE_SIZE, PAGE_SIZE, n_heads, HEAD_DIM)

    return (y_p.reshape(bp, sp, d), y_s.reshape(bs, ss, d), wkv_p, shift_p, paged(k_p), paged(v_p),
            wkv_s, shift_s, k_s.reshape(bs, ss, n_heads, HEAD_DIM), v_s.reshape(bs, ss, n_heads, HEAD_DIM))
```
